```python
import jax, jax.numpy as jnp
from jax import lax
import numpy as np

D_MODEL = 1024
BATCH = 8
SEQ = 2048
DEPTH = 1
DEC_BATCH = 128
DEC_SEQ = 4
PAST_LEN = 2048
PAGE_SIZE = 128

HEAD_DIM = 64
HEADS_PER_GROUP = 8
GROUP_WINDOWS = (128, 512, 2048)
GROUP_DILATIONS = (1, 4, 16)
N_GROUPS = 3
ATTN_HEADS = N_GROUPS * HEADS_PER_GROUP
ATTN_WIDTH = ATTN_HEADS * HEAD_DIM
ATTN_OUT_WIDTH = HEADS_PER_GROUP * HEAD_DIM
ROPE_THETA = 10000.0
CONV_DIM = D_MODEL
CONV_WIDTH = 31
N_BRANCHES = 2
IN_COLS = 3 * ATTN_WIDTH + 2 * CONV_DIM + N_BRANCHES * D_MODEL
PEER_HEADS = 8
N_KEYS = 128
N_EXPERTS = N_KEYS * N_KEYS
PEER_TOPK = 16
PEER_QDIM = 256
PEER_HALF = PEER_QDIM // 2
PEER_BLOCK = 256
RMS_EPS = 1e-6
LN_EPS = 1e-5

kernel_name = 'conformer_dilated_peer_hybrid_step'

F32 = jnp.float32


def rms_norm(x, w):
    xf = x.astype(F32)
    y = xf * lax.rsqrt(jnp.mean(xf * xf, axis=-1, keepdims=True) + RMS_EPS)
    return (y * w.astype(F32)).astype(x.dtype)


def layer_norm(x, g, b):
    xf = x.astype(F32)
    mu = jnp.mean(xf, axis=-1, keepdims=True)
    var = jnp.mean(jnp.square(xf - mu), axis=-1, keepdims=True)
    y = (xf - mu) * lax.rsqrt(var + LN_EPS)
    return (y * g.astype(F32) + b.astype(F32)).astype(x.dtype)


def rotary(x, pos):
    half = HEAD_DIM // 2
    inv_freq = ROPE_THETA ** (-jnp.arange(half, dtype=F32) / half)
    ang = pos.astype(F32)[:, None] * inv_freq[None, :]
    cos = jnp.cos(ang)[:, None, :]
    sin = jnp.sin(ang)[:, None, :]
    xf = x.astype(F32)
    x1, x2 = xf[..., :half], xf[..., half:]
    return jnp.concatenate([x1 * cos - x2 * sin, x2 * cos + x1 * sin], axis=-1).astype(x.dtype)


def mixer_inputs(x, pos, norm1_w, w_in, q_norm_w, k_norm_w):
    b, t, _ = x.shape
    h = rms_norm(x, norm1_w)
    z = h @ w_in
    q, k, v, glu_in, gate_in = jnp.split(
        z, [ATTN_WIDTH, 2 * ATTN_WIDTH, 3 * ATTN_WIDTH, 3 * ATTN_WIDTH + 2 * CONV_DIM], axis=-1)
    q = rotary(rms_norm(q.reshape(b, t, ATTN_HEADS, HEAD_DIM), q_norm_w), pos)
    k = rotary(rms_norm(k.reshape(b, t, ATTN_HEADS, HEAD_DIM), k_norm_w), pos)
    v = v.reshape(b, t, ATTN_HEADS, HEAD_DIM)
    a, g = jnp.split(glu_in, 2, axis=-1)
    u = a * jax.nn.sigmoid(g)
    gates = jax.nn.sigmoid(gate_in.astype(F32)).astype(x.dtype).reshape(b, t, N_BRANCHES, D_MODEL)
    return q, k, v, u, gates


def conv_branch(u, left, conv_w, conv_b, ln_g, ln_b, w_conv_proj):
    xp = jnp.concatenate([left, u], axis=1)
    y = lax.conv_general_dilated(
        xp, conv_w[:, None, :], window_strides=(1,), padding='VALID',
        dimension_numbers=('NWC', 'WIO', 'NWC'), feature_group_count=CONV_DIM)
    y = layer_norm(y + conv_b, ln_g, ln_b)
    y = jax.nn.silu(y)
    return y @ w_conv_proj, xp[:, -(CONV_WIDTH - 1):]


def dilated_group_prompt(q, k, v, window, dilation):
    b, t, h, e = q.shape
    m = t // dilation
    steps = window // dilation
    blk = min(steps, m)
    nb = -(-m // blk)
    mp = nb * blk

    def to_res(a):
        a = a.reshape(b, m, dilation, h, e).transpose(0, 2, 1, 3, 4)
        return jnp.pad(a, ((0, 0), (0, 0), (0, mp - m), (0, 0), (0, 0)))

    qr, kr, vr = to_res(q), to_res(k), to_res(v)
    qb = qr.reshape(b, dilation, nb, blk, h, e)

    def band(a):
        prev = jnp.pad(a, ((0, 0), (0, 0), (blk, 0), (0, 0), (0, 0)))[:, :, :mp]
        return jnp.concatenate([prev.reshape(b, dilation, nb, blk, h, e),
                                a.reshape(b, dilation, nb, blk, h, e)], axis=3)

    kb, vb = band(kr), band(vr)
    qi = jnp.arange(blk)[:, None]
    kj = jnp.arange(2 * blk)[None, :]
    diff = blk + qi - kj
    kpos = (jnp.arange(nb)[:, None, None] - 1) * blk + kj[None]
    mask = (diff >= 0)[None] & (diff <= steps)[None] & (kpos >= 0)
    s = jnp.einsum('brnqhe,brnkhe->brnhqk', qb, kb, preferred_element_type=F32) * (HEAD_DIM ** -0.5)
    s = jnp.where(mask[None, None, :, None], s, -jnp.inf)
    lse = jax.nn.logsumexp(s, axis=-1)
    p = jnp.exp(s - lse[..., None]).astype(v.dtype)
    o = jnp.einsum('brnhqk,brnkhe->brnqhe', p, vb).reshape(b, dilation, mp, h, e)[:, :, :m]
    o = o.transpose(0, 2, 1, 3, 4).reshape(b, t, h, e)
    lse = lse.transpose(0, 1, 2, 4, 3).reshape(b, dilation, mp, h)[:, :, :m]
    lse = lse.transpose(0, 2, 1, 3).reshape(b, t, h)
    return o, lse


def dilated_group_sample(q, k_new, v_new, k_buf, v_buf, window, dilation):
    L = k_buf.shape[1]
    s_len = q.shape[1]
    steps = window // dilation
    keys = jnp.concatenate([k_buf, k_new], axis=1)
    vals = jnp.concatenate([v_buf, v_new], axis=1)
    idx = L + jnp.arange(s_len)[:, None] - jnp.arange(steps + 1)[None, :] * dilation
    valid = idx >= 0
    idx = jnp.maximum(idx, 0)
    kg = keys[:, idx]
    vg = vals[:, idx]
    s = jnp.einsum('bshe,bsjhe->bhsj', q, kg, preferred_element_type=F32) * (HEAD_DIM ** -0.5)
    s = jnp.where(valid[None, None], s, -jnp.inf)
    lse = jax.nn.logsumexp(s, axis=-1)
    p = jnp.exp(s - lse[..., None]).astype(v_new.dtype)
    o = jnp.einsum('bhsj,bsjhe->bshe', p, vg)
    return o, lse.transpose(0, 2, 1)


def combine_groups(outs, lses):
    w = jax.nn.softmax(jnp.stack(lses, axis=0), axis=0)
    o = jnp.sum(w[..., None] * jnp.stack(outs, axis=0).astype(F32), axis=0)
    return o.astype(outs[0].dtype)


def attention_prompt(q, k, v):
    outs, lses, states = [], [], []
    t = q.shape[1]
    for g in range(N_GROUPS):
        sl = slice(g * HEADS_PER_GROUP, (g + 1) * HEADS_PER_GROUP)
        o, l = dilated_group_prompt(q[:, :, sl], k[:, :, sl], v[:, :, sl],
                                    GROUP_WINDOWS[g], GROUP_DILATIONS[g])
        outs.append(o)
        lses.append(l)
        keep = min(GROUP_WINDOWS[g], t)
        states.append(jnp.stack([k[:, t - keep:, sl], v[:, t - keep:, sl]], axis=2))
    return combine_groups(outs, lses), states


def attention_sample(q, k, v, caches):
    outs, lses, rows = [], [], []
    for g in range(N_GROUPS):
        sl = slice(g * HEADS_PER_GROUP, (g + 1) * HEADS_PER_GROUP)
        o, l = dilated_group_sample(q[:, :, sl], k[:, :, sl], v[:, :, sl],
                                    caches[g][:, :, 0], caches[g][:, :, 1],
                                    GROUP_WINDOWS[g], GROUP_DILATIONS[g])
        outs.append(o)
        lses.append(l)
        rows.append(jnp.stack([k[:, :, sl], v[:, :, sl]], axis=2))
    return combine_groups(outs, lses), rows


def peer(h, w_peer_q, sub_keys, expert_u, expert_v):
    n = h.shape[0]
    n_pad = -(-n // PEER_BLOCK) * PEER_BLOCK
    hp = jnp.pad(h, ((0, n_pad - n), (0, 0))).reshape(n_pad // PEER_BLOCK, PEER_BLOCK, D_MODEL)

    def block_fn(hb):
        q = (hb @ w_peer_q).reshape(PEER_BLOCK, PEER_HEADS, 2, PEER_HALF)
        s = jnp.einsum('thpc,hpkc->thpk', q, sub_keys, preferred_element_type=F32)
        s1, i1 = lax.top_k(s[:, :, 0], PEER_TOPK)
        s2, i2 = lax.top_k(s[:, :, 1], PEER_TOPK)
        cand = (s1[..., :, None] + s2[..., None, :]).reshape(PEER_BLOCK, PEER_HEADS, PEER_TOPK * PEER_TOPK)
        top_s, top_c = lax.top_k(cand, PEER_TOPK)
        e1 = jnp.take_along_axis(i1, top_c // PEER_TOPK, axis=-1)
        e2 = jnp.take_along_axis(i2, top_c % PEER_TOPK, axis=-1)
        idx = (e1 * N_KEYS + e2).reshape(PEER_BLOCK, PEER_HEADS * PEER_TOPK)
        g = jax.nn.softmax(top_s, axis=-1).reshape(PEER_BLOCK, PEER_HEADS * PEER_TOPK)
        u_rows = jnp.take(expert_u, idx, axis=0)
        a = jnp.einsum('td,tkd->tk', hb, u_rows, preferred_element_type=F32)
        wgt = (g * jax.nn.gelu(a, approximate=False)).astype(hb.dtype)
        v_rows = jnp.take(expert_v, idx, axis=0)
        return jnp.einsum('tk,tkd->td', wgt, v_rows)

    return lax.map(block_fn, hp).reshape(n_pad, D_MODEL)[:n]


def finish_layer(x, heads_out, conv_out, gates, w_attn_proj, w_out, norm2_w,
                 w_peer_q, peer_sub_keys, expert_u, expert_v):
    b, t, _ = x.shape
    attn_out = heads_out.reshape(b, t, ATTN_OUT_WIDTH) @ w_attn_proj
    mixed = gates[:, :, 0] * conv_out + gates[:, :, 1] * attn_out
    x1 = x + mixed @ w_out
    h2 = rms_norm(x1, norm2_w)
    return x1 + peer(h2.reshape(b * t, D_MODEL), w_peer_q, peer_sub_keys,
                     expert_u, expert_v).reshape(b, t, D_MODEL)


def setup_inputs(seed: int = 0) -> dict:
    key = jax.random.key(seed)
    ks = jax.random.split(key, 24)

    def nrm(k, shape, scale):
        return jax.random.normal(k, shape, F32) * scale

    lens = [min(w, PAST_LEN) for w in GROUP_WINDOWS]
    return {
        'x_prompt': nrm(ks[0], (BATCH, SEQ, D_MODEL), 1.0),
        'x_sample': nrm(ks[1], (DEC_BATCH, DEC_SEQ, D_MODEL), 1.0),
        'cache_kv_g1': nrm(ks[2], (DEPTH, DEC_BATCH, lens[0], 2, HEADS_PER_GROUP, HEAD_DIM), 1.0),
        'cache_kv_g2': nrm(ks[3], (DEPTH, DEC_BATCH, lens[1], 2, HEADS_PER_GROUP, HEAD_DIM), 1.0),
        'cache_kv_g3': nrm(ks[4], (DEPTH, DEC_BATCH, lens[2], 2, HEADS_PER_GROUP, HEAD_DIM), 1.0),
        'state_conv': nrm(ks[5], (DEPTH, DEC_BATCH, CONV_WIDTH - 1, CONV_DIM), 0.5),
        'norm1_w': 1.0 + nrm(ks[6], (DEPTH, D_MODEL), 0.02),
        'w_in': nrm(ks[7], (DEPTH, D_MODEL, IN_COLS), D_MODEL ** -0.5),
        'q_norm_w': 1.0 + nrm(ks[8], (DEPTH, HEAD_DIM), 0.02),
        'k_norm_w': 1.0 + nrm(ks[9], (DEPTH, HEAD_DIM), 0.02),
        'conv_w': nrm(ks[10], (DEPTH, CONV_WIDTH, CONV_DIM), CONV_WIDTH ** -0.5),
        'conv_b': nrm(ks[11], (DEPTH, CONV_DIM), 0.02),
        'conv_ln_g': 1.0 + nrm(ks[12], (DEPTH, CONV_DIM), 0.02),
        'conv_ln_b': nrm(ks[13], (DEPTH, CONV_DIM), 0.02),
        'w_conv_proj': nrm(ks[14], (DEPTH, CONV_DIM, D_MODEL), CONV_DIM ** -0.5),
        'w_attn_proj': nrm(ks[15], (DEPTH, ATTN_OUT_WIDTH, D_MODEL), ATTN_OUT_WIDTH ** -0.5),
        'w_out': nrm(ks[16], (DEPTH, D_MODEL, D_MODEL), D_MODEL ** -0.5),
        'norm2_w': 1.0 + nrm(ks[17], (DEPTH, D_MODEL), 0.02),
        'w_peer_q': nrm(ks[18], (DEPTH, D_MODEL, PEER_HEADS * PEER_QDIM), D_MODEL ** -0.5),
        'peer_sub_keys': nrm(ks[19], (DEPTH, PEER_HEADS, 2, N_KEYS, PEER_HALF), PEER_HALF ** -0.5),
        'expert_u': nrm(ks[20], (DEPTH, N_EXPERTS, D_MODEL), D_MODEL ** -0.5),
        'expert_v': nrm(ks[21], (DEPTH, N_EXPERTS, D_MODEL), PEER_HEADS ** -0.5),
    }


def reference(x_prompt, x_sample, cache_kv_g1, cache_kv_g2, cache_kv_g3, state_conv,
              norm1_w, w_in, q_norm_w, k_norm_w, conv_w, conv_b, conv_ln_g, conv_ln_b,
              w_conv_proj, w_attn_proj, w_out, norm2_w, w_peer_q, peer_sub_keys,
              expert_u, expert_v):
    pos_p = jnp.arange(x_prompt.shape[1], dtype=jnp.int32)
    pos_s = PAST_LEN + jnp.arange(x_sample.shape[1], dtype=jnp.int32)
    caches = (cache_kv_g1, cache_kv_g2, cache_kv_g3)
    hp, hs = x_prompt, x_sample
    kv_p = [[], [], []]
    kv_s = [[], [], []]
    conv_p, conv_s = [], []
    for layer in range(DEPTH):
        conv_args = (conv_w[layer], conv_b[layer], conv_ln_g[layer], conv_ln_b[layer], w_conv_proj[layer])
        tail_args = (w_attn_proj[layer], w_out[layer], norm2_w[layer], w_peer_q[layer],
                     peer_sub_keys[layer], expert_u[layer], expert_v[layer])
        q, k, v, u, gates = mixer_inputs(hp, pos_p, norm1_w[layer], w_in[layer], q_norm_w[layer], k_norm_w[layer])
        left = jnp.zeros((hp.shape[0], CONV_WIDTH - 1, CONV_DIM), u.dtype)
        c_out, c_state = conv_branch(u, left, *conv_args)
        a_out, kv_states = attention_prompt(q, k, v)
        hp = finish_layer(hp, a_out, c_out, gates, *tail_args)
        conv_p.append(c_state)
        for g in range(N_GROUPS):
            kv_p[g].append(kv_states[g])
        q, k, v, u, gates = mixer_inputs(hs, pos_s, norm1_w[layer], w_in[layer], q_norm_w[layer], k_norm_w[layer])
        c_out, c_state = conv_branch(u, state_conv[layer], *conv_args)
        a_out, kv_rows = attention_sample(q, k, v, [c[layer] for c in caches])
        hs = finish_layer(hs, a_out, c_out, gates, *tail_args)
        conv_s.append(c_state)
        for g in range(N_GROUPS):
            kv_s[g].append(kv_rows[g])
    return (hp, hs,
            jnp.stack(kv_p[0]), jnp.stack(kv_p[1]), jnp.stack(kv_p[2]), jnp.stack(conv_p),
            jnp.stack(kv_s[0]), jnp.stack(kv_s[1]), jnp.stack(kv_s[2]), jnp.stack(conv_s))
```

```python
import functools

import numpy as np
import jax
import jax.numpy as jnp
from jax import lax
from jax.experimental import pallas as pl
from jax.experimental.pallas import tpu as pltpu

F32 = jnp.float32
BF16 = jnp.bfloat16

D_MODEL = 1024
HEAD_DIM = 64
HALF_DIM = HEAD_DIM // 2
GROUP_HEADS = 8
N_GROUPS = 3
GROUP_WIDTH = GROUP_HEADS * HEAD_DIM
ATTN_WIDTH = N_GROUPS * GROUP_WIDTH
GROUP_WINDOWS = (128, 512, 2048)
GROUP_DILATIONS = (1, 4, 16)
WINDOW_STEPS = 128
ROPE_THETA = 10000.0
CONV_DIM = D_MODEL
CONV_WIDTH = 31
CONV_HALO = 32
PEER_HEADS = 8
N_KEYS = 128
N_EXPERTS = N_KEYS * N_KEYS
PEER_TOPK = 16
PEER_HALF = 128
PEER_QCOLS = PEER_HEADS * 2 * PEER_HALF
RMS_EPS = 1e-6
LN_EPS = 1e-5
NEG_INF = float("-inf")

VMEM_LIMIT = 56 * 1024 * 1024


def _params(*sem):
    return pltpu.CompilerParams(dimension_semantics=sem, vmem_limit_bytes=VMEM_LIMIT)


def _full(shape):
    return pl.BlockSpec(shape, lambda *_: (0,) * len(shape), pipeline_mode=pl.Buffered(1))


def _split_bf16(x):
    hi = x.astype(BF16)
    lo = (x - hi.astype(F32)).astype(BF16)
    return hi, lo


def _rope_tables(positions):
    inv_freq = ROPE_THETA ** (-np.arange(HALF_DIM, dtype=np.float64) / HALF_DIM)
    ang = np.asarray(positions, np.float64)[:, None] * inv_freq[None, :]
    cos = np.cos(ang)
    sin = np.sin(ang)
    cos_t = np.concatenate([cos, cos, cos, cos], axis=1)
    sin_t = np.concatenate([-sin, sin, -sin, sin], axis=1)
    return jnp.asarray(cos_t, F32), jnp.asarray(sin_t, F32)


def _head_mean_matrix():
    idx = np.arange(GROUP_WIDTH) // HEAD_DIM
    return jnp.asarray((idx[:, None] == idx[None, :]) / HEAD_DIM, BF16)


def _head_indicator():
    idx = np.arange(GROUP_WIDTH) // HEAD_DIM
    return jnp.asarray(idx[:, None] == np.arange(128)[None, :], BF16)


def _mixer_in_kernel(x_ref, n1_ref, w_ref, cos_ref, sin_ref, qn_ref, kn_ref, seg_ref,
                     q_ref, k_ref, v_ref, u_ref, g_ref, *, q_scale):
    tm = x_ref.shape[0]
    x = x_ref[...]
    ms = jnp.mean(x * x, axis=-1, keepdims=True)
    h = (x * lax.rsqrt(ms + RMS_EPS) * n1_ref[...]).astype(BF16)
    cos = jnp.concatenate([cos_ref[...]] * 4, axis=1)
    sin = jnp.concatenate([sin_ref[...]] * 4, axis=1)
    lane = lax.broadcasted_iota(jnp.int32, (tm, GROUP_WIDTH), 1)
    first_half = (lane % HEAD_DIM) < HALF_DIM

    def proj(c0, width):
        return jnp.dot(h, w_ref[:, c0:c0 + width], preferred_element_type=F32)

    def normed_rotated(c0, nw_ref):
        z = proj(c0, GROUP_WIDTH)
        msq = jnp.dot((z * z).astype(BF16), seg_ref[...], preferred_element_type=F32)
        zn = z * lax.rsqrt(msq + RMS_EPS) * nw_ref[...]
        swapped = jnp.where(first_half,
                            pltpu.roll(zn, GROUP_WIDTH - HALF_DIM, 1),
                            pltpu.roll(zn, HALF_DIM, 1))
        return zn * cos + swapped * sin

    for g in range(N_GROUPS):
        cs = slice(g * GROUP_WIDTH, (g + 1) * GROUP_WIDTH)
        q_ref[:, cs] = (normed_rotated(g * GROUP_WIDTH, qn_ref) * q_scale).astype(q_ref.dtype)
        k_ref[:, cs] = normed_rotated(ATTN_WIDTH + g * GROUP_WIDTH, kn_ref)
        v_ref[:, cs] = proj(2 * ATTN_WIDTH + g * GROUP_WIDTH, GROUP_WIDTH)
    glu0 = 3 * ATTN_WIDTH
    for c in range(CONV_DIM // GROUP_WIDTH):
        cs = slice(c * GROUP_WIDTH, (c + 1) * GROUP_WIDTH)
        a = proj(glu0 + c * GROUP_WIDTH, GROUP_WIDTH)
        b = proj(glu0 + CONV_DIM + c * GROUP_WIDTH, GROUP_WIDTH)
        u_ref[:, cs] = a * jax.nn.sigmoid(b)
    gate0 = glu0 + 2 * CONV_DIM
    for c in range(2 * D_MODEL // GROUP_WIDTH):
        cs = slice(c * GROUP_WIDTH, (c + 1) * GROUP_WIDTH)
        g_ref[:, cs] = jax.nn.sigmoid(proj(gate0 + c * GROUP_WIDTH, GROUP_WIDTH))


def _mixer_in(x2d, positions_period, cos_t, sin_t, n1w, w_in_bf, qnw, knw, seg, *, tm, q_dtype, q_scale):
    n = x2d.shape[0]
    in_cols = w_in_bf.shape[1]
    period_tiles = positions_period // tm
    row = lambda i: (i, 0)
    return pl.pallas_call(
        functools.partial(_mixer_in_kernel, q_scale=q_scale),
        grid=(n // tm,),
        in_specs=[
            pl.BlockSpec((tm, D_MODEL), row),
            _full((1, D_MODEL)),
            _full((D_MODEL, in_cols)),
            pl.BlockSpec((tm, 128), lambda i: (i % period_tiles, 0)),
            pl.BlockSpec((tm, 128), lambda i: (i % period_tiles, 0)),
            _full((1, GROUP_WIDTH)),
            _full((1, GROUP_WIDTH)),
            _full((GROUP_WIDTH, GROUP_WIDTH)),
        ],
        out_specs=[
            pl.BlockSpec((tm, ATTN_WIDTH), row),
            pl.BlockSpec((tm, ATTN_WIDTH), row),
            pl.BlockSpec((tm, ATTN_WIDTH), row),
            pl.BlockSpec((tm, CONV_DIM), row),
            pl.BlockSpec((tm, 2 * D_MODEL), row),
        ],
        out_shape=[
            jax.ShapeDtypeStruct((n, ATTN_WIDTH), q_dtype),
            jax.ShapeDtypeStruct((n, ATTN_WIDTH), F32),
            jax.ShapeDtypeStruct((n, ATTN_WIDTH), F32),
            jax.ShapeDtypeStruct((n, CONV_DIM), F32),
            jax.ShapeDtypeStruct((n, 2 * D_MODEL), F32),
        ],
        compiler_params=_params("parallel"),
        name="mixer_in",
    )(x2d, n1w, w_in_bf, cos_t, sin_t, qnw, knw, seg)


CONV_ROWS = 32
CONV_LANES = 256


def _conv_tail(y, b_ref, lg_ref, lb_ref, wp_ref):
    y = y + b_ref[...]
    mu = jnp.mean(y, axis=-1, keepdims=True)
    yc = y - mu
    var = jnp.mean(yc * yc, axis=-1, keepdims=True)
    yn = yc * lax.rsqrt(var + LN_EPS) * lg_ref[...] + lb_ref[...]
    act = yn * jax.nn.sigmoid(yn)
    return jnp.dot(act.astype(BF16), wp_ref[...], preferred_element_type=F32)


def _conv_prompt_kernel(u_ref, prev_ref, cw_ref, b_ref, lg_ref, lb_ref, wp_ref, o_ref, xs_ref, y_ref):
    tm = u_ref.shape[0]
    i = pl.program_id(1)
    prev = prev_ref[...]
    xs_ref[0:CONV_HALO, :] = jnp.where(i == 0, jnp.zeros_like(prev), prev)
    xs_ref[CONV_HALO:, :] = u_ref[...]
    first = CONV_HALO - (CONV_WIDTH - 1)

    def rows(r, carry):
        base = pl.multiple_of(r * CONV_ROWS, CONV_ROWS)
        for c in range(CONV_DIM // CONV_LANES):
            cs = slice(c * CONV_LANES, (c + 1) * CONV_LANES)
            window = xs_ref[pl.ds(base, CONV_ROWS + CONV_HALO), cs]
            acc = jnp.zeros((CONV_ROWS, CONV_LANES), F32)
            for w in range(CONV_WIDTH):
                acc = acc + window[first + w:first + w + CONV_ROWS, :] * cw_ref[w:w + 1, cs]
            y_ref[pl.ds(base, CONV_ROWS), cs] = acc
        return carry

    lax.fori_loop(0, tm // CONV_ROWS, rows, 0)
    o_ref[...] = _conv_tail(y_ref[...], b_ref, lg_ref, lb_ref, wp_ref)


def _conv_prompt(u3d, cw, cb, lg, lb, wp_bf, *, tm):
    bsz, t, _ = u3d.shape
    halo_per_tile = tm // CONV_HALO
    return pl.pallas_call(
        _conv_prompt_kernel,
        grid=(bsz, t // tm),
        in_specs=[
            pl.BlockSpec((None, tm, CONV_DIM), lambda b, i: (b, i, 0)),
            pl.BlockSpec((None, CONV_HALO, CONV_DIM),
                         lambda b, i: (b, jnp.maximum(i * halo_per_tile - 1, 0), 0)),
            _full((CONV_WIDTH, CONV_DIM)),
            _full((1, CONV_DIM)), _full((1, CONV_DIM)), _full((1, CONV_DIM)),
            _full((CONV_DIM, D_MODEL)),
        ],
        out_specs=pl.BlockSpec((None, tm, D_MODEL), lambda b, i: (b, i, 0)),
        out_shape=jax.ShapeDtypeStruct((bsz, t, D_MODEL), F32),
        scratch_shapes=[pltpu.VMEM((tm + CONV_HALO, CONV_DIM), F32),
                        pltpu.VMEM((tm, CONV_DIM), F32)],
        compiler_params=_params("parallel", "arbitrary"),
        name="conv_prompt",
    )(u3d, u3d, cw, cb, lg, lb, wp_bf)


def _conv_sample_kernel(xp_ref, cw_ref, b_ref, lg_ref, lb_ref, wp_ref, o_ref):
    n_new = o_ref.shape[0]
    for s in range(n_new):
        acc = jnp.zeros(xp_ref.shape[1:], F32)
        for w in range(CONV_WIDTH):
            acc = acc + xp_ref[s + w] * cw_ref[w:w + 1, :]
        o_ref[s] = _conv_tail(acc, b_ref, lg_ref, lb_ref, wp_ref)


def _conv_sample(xp_t, cw, cb, lg, lb, wp_bf):
    rows, bd, _ = xp_t.shape
    n_new = rows - (CONV_WIDTH - 1)
    return pl.pallas_call(
        _conv_sample_kernel,
        grid=(1,),
        in_specs=[_full(xp_t.shape), _full((CONV_WIDTH, CONV_DIM)),
                  _full((1, CONV_DIM)), _full((1, CONV_DIM)), _full((1, CONV_DIM)),
                  _full((CONV_DIM, D_MODEL))],
        out_specs=pl.BlockSpec((n_new, bd, D_MODEL), lambda i: (0, 0, 0)),
        out_shape=jax.ShapeDtypeStruct((n_new, bd, D_MODEL), F32),
        compiler_params=_params("arbitrary"),
        name="conv_sample",
    )(xp_t, cw, cb, lg, lb, wp_bf)


def _attn_prompt_kernel(q_ref, kp_ref, kc_ref, vp_ref, vc_ref, o_ref, l_ref):
    blk = q_ref.shape[0]
    n = pl.program_id(2)
    qi = lax.broadcasted_iota(jnp.int32, (blk, 2 * blk), 0)
    kj = lax.broadcasted_iota(jnp.int32, (blk, 2 * blk), 1)
    first_key = jnp.where(n > 0, qi, blk)
    valid = (kj >= first_key) & (kj <= qi + blk)
    outs, lses = [], []
    for h in range(GROUP_HEADS):
        cs = slice(h * HEAD_DIM, (h + 1) * HEAD_DIM)
        q = q_ref[:, cs]
        k = jnp.concatenate([kp_ref[:, cs], kc_ref[:, cs]], axis=0).astype(BF16)
        v = jnp.concatenate([vp_ref[:, cs], vc_ref[:, cs]], axis=0).astype(BF16)
        s = lax.dot_general(q, k, (((1,), (1,)), ((), ())), preferred_element_type=F32)
        s = jnp.where(valid, s, NEG_INF)
        m = jnp.max(s, axis=-1, keepdims=True)
        p = jnp.exp(s - m)
        l = jnp.sum(p, axis=-1, keepdims=True)
        o = jnp.dot(p.astype(BF16), v, preferred_element_type=F32) / l
        outs.append(o)
        lses.append(jnp.broadcast_to(m + jnp.log(l), (blk, HEAD_DIM)))
    o_ref[...] = jnp.concatenate(outs, axis=1)
    l_ref[...] = jnp.concatenate(lses, axis=1)


def _attn_prompt_group(q3d, k3d, v3d, g):
    bsz, t, _ = q3d.shape
    d = GROUP_DILATIONS[g]
    m = t // d
    blk = WINDOW_STEPS
    assert m % blk == 0
    nb = m // blk
    view = lambda a: a.reshape(bsz, m, d * ATTN_WIDTH)
    cur = lambda b, r, n: (b, n, r * N_GROUPS + g)
    prv = lambda b, r, n: (b, jnp.maximum(n - 1, 0), r * N_GROUPS + g)
    blk_spec = lambda im: pl.BlockSpec((None, blk, GROUP_WIDTH), im)
    o, lse = pl.pallas_call(
        _attn_prompt_kernel,
        grid=(bsz, d, nb),
        in_specs=[blk_spec(cur), blk_spec(prv), blk_spec(cur), blk_spec(prv), blk_spec(cur)],
        out_specs=[pl.BlockSpec((None, blk, GROUP_WIDTH), lambda b, r, n: (b, n, r))] * 2,
        out_shape=[jax.ShapeDtypeStruct((bsz, m, d * GROUP_WIDTH), F32)] * 2,
        compiler_params=_params("parallel", "parallel", "arbitrary"),
        name=f"attn_prompt_g{g + 1}",
    )(view(q3d), view(k3d), view(k3d), view(v3d), view(v3d))
    return o.reshape(bsz, t, GROUP_WIDTH), lse.reshape(bsz, t, GROUP_WIDTH)


def _combine_kernel(o1, o2, o3, l1, l2, l3, out_ref):
    la, lb, lc = l1[...], l2[...], l3[...]
    mx = jnp.maximum(jnp.maximum(la, lb), lc)
    ea, eb, ec = jnp.exp(la - mx), jnp.exp(lb - mx), jnp.exp(lc - mx)
    out_ref[...] = (ea * o1[...] + eb * o2[...] + ec * o3[...]) / (ea + eb + ec)


def _combine_groups(outs, lses, *, tm):
    n = outs[0].shape[0]
    spec = pl.BlockSpec((tm, GROUP_WIDTH), lambda i: (i, 0))
    return pl.pallas_call(
        _combine_kernel,
        grid=(n // tm,),
        in_specs=[spec] * 6,
        out_specs=spec,
        out_shape=jax.ShapeDtypeStruct((n, GROUP_WIDTH), F32),
        compiler_params=_params("parallel"),
        name="combine_groups",
    )(*outs, *lses)


def _attn_sample_kernel(q_ref, kn_ref, vn_ref, c1_ref, c2_ref, c3_ref, ind_ref, indt_ref, o_ref,
                        m_scr, l_scr, o_scr, *, n_new):
    bb = c1_ref.shape[0]
    ind = ind_ref[...]
    indt = indt_ref[...]
    key_row = lax.broadcasted_iota(jnp.int32, (WINDOW_STEPS, 128), 0)
    new_row = lax.broadcasted_iota(jnp.int32, (8, 128), 0)
    caches = (c1_ref, c2_ref, c3_ref)

    def head_sums(x):
        hi, lo = _split_bf16(x)
        return (jnp.dot(hi, ind, preferred_element_type=F32)
                + jnp.dot(lo, ind, preferred_element_type=F32))

    for b in range(bb):
        r8 = (b * n_new // 8) * 8
        for g in range(N_GROUPS):
            gs = slice(g * GROUP_WIDTH, (g + 1) * GROUP_WIDTH)
            k_new = kn_ref[r8:r8 + 8, gs]
            v_new = vn_ref[r8:r8 + 8, gs]
            for s in range(n_new):
                row = b * n_new + s
                q = q_ref[row:row + 1, gs]
                col = 0 if g == 0 else s * 2 * GROUP_WIDTH
                k_old = caches[g][b, :, col:col + GROUP_WIDTH]
                v_old = caches[g][b, :, col + GROUP_WIDTH:col + 2 * GROUP_WIDTH]
                sc_old = head_sums(k_old * q)
                sc_new = head_sums(k_new * q)
                local = new_row - (row - r8)
                if g == 0:
                    ok_old = key_row >= s
                    ok_new = (local <= 0) & (local >= -s)
                else:
                    ok_old = key_row >= 0
                    ok_new = local == 0
                sc_old = jnp.where(ok_old, sc_old, NEG_INF)
                sc_new = jnp.where(ok_new, sc_new, NEG_INF)
                mx = jnp.maximum(jnp.max(sc_old, axis=0, keepdims=True),
                                 jnp.max(sc_new, axis=0, keepdims=True))
                p_old = jnp.exp(sc_old - mx)
                p_new = jnp.exp(sc_new - mx)
                den = jnp.sum(p_old, axis=0, keepdims=True) + jnp.sum(p_new, axis=0, keepdims=True)
                pe_old = jnp.dot(p_old.astype(BF16), indt, preferred_element_type=F32)
                pe_new = jnp.dot(p_new.astype(BF16), indt, preferred_element_type=F32)
                acc = (jnp.sum(pe_old * v_old, axis=0, keepdims=True)
                       + jnp.sum(pe_new * v_new, axis=0, keepdims=True))
                m_scr[g, row:row + 1, :] = mx
                l_scr[g, row:row + 1, :] = den
                o_scr[g, row:row + 1, :] = acc

    m_all = m_scr[...]
    mx = jnp.max(m_all, axis=0)
    scale = jnp.exp(m_all - mx[None])
    total = jnp.sum(scale * l_scr[...], axis=0)
    out = jnp.zeros(o_ref.shape, F32)
    for g in range(N_GROUPS):
        hi, lo = _split_bf16(scale[g] / total)
        coef = (jnp.dot(hi, indt, preferred_element_type=F32)
                + jnp.dot(lo, indt, preferred_element_type=F32))
        out = out + coef * o_scr[g]
    o_ref[...] = out


def _attn_sample(q2d, k2d, v2d, cache1, cache2, cache3, ind, indt, *, n_new, bb):
    rows = q2d.shape[0]
    bd = rows // n_new
    row_w = 2 * GROUP_WIDTH
    views = []
    for g, c in enumerate((cache1, cache2, cache3)):
        d = GROUP_DILATIONS[g]
        assert c.shape[1] == GROUP_WINDOWS[g]
        views.append(c.reshape(bd, WINDOW_STEPS, d * row_w))
    rblk = bb * n_new
    rspec = pl.BlockSpec((rblk, ATTN_WIDTH), lambda i: (i, 0))
    cspec = lambda w: pl.BlockSpec((bb, WINDOW_STEPS, w), lambda i: (i, 0, 0))
    return pl.pallas_call(
        functools.partial(_attn_sample_kernel, n_new=n_new),
        grid=(bd // bb,),
        in_specs=[rspec, rspec, rspec, cspec(row_w), cspec(n_new * row_w), cspec(n_new * row_w),
                  _full((GROUP_WIDTH, 128)), _full((128, GROUP_WIDTH))],
        out_specs=pl.BlockSpec((rblk, GROUP_WIDTH), lambda i: (i, 0)),
        out_shape=jax.ShapeDtypeStruct((rows, GROUP_WIDTH), F32),
        scratch_shapes=[pltpu.VMEM((N_GROUPS, rblk, 128), F32),
                        pltpu.VMEM((N_GROUPS, rblk, 128), F32),
                        pltpu.VMEM((N_GROUPS, rblk, GROUP_WIDTH), F32)],
        compiler_params=_params("parallel"),
        name="attn_sample",
    )(q2d, k2d, v2d, *views, ind, indt)


def _finish_kernel(x_ref, heads_ref, conv_ref, gate_ref, wap_ref, wout_ref, n2_ref, wpq_ref, sk_ref,
                   x1_ref, h2_ref, st_ref):
    attn = jnp.dot(heads_ref[...].astype(BF16), wap_ref[...], preferred_element_type=F32)
    mixed = gate_ref[:, :D_MODEL] * conv_ref[...] + gate_ref[:, D_MODEL:] * attn
    x1 = x_ref[...] + jnp.dot(mixed.astype(BF16), wout_ref[...], preferred_element_type=F32)
    x1_ref[...] = x1
    ms = jnp.mean(x1 * x1, axis=-1, keepdims=True)
    h2 = (x1 * lax.rsqrt(ms + RMS_EPS) * n2_ref[...]).astype(BF16)
    h2_ref[...] = h2
    qp = jnp.dot(h2, wpq_ref[...], preferred_element_type=F32).astype(BF16)
    for c in range(PEER_QCOLS // PEER_HALF):
        cs = slice(c * PEER_HALF, (c + 1) * PEER_HALF)
        st_ref[cs, :] = lax.dot_general(sk_ref[c], qp[:, cs], (((1,), (1,)), ((), ())),
                                        preferred_element_type=F32)


def _finish(x2d, heads, conv_out, gates, wap_bf, wout_bf, n2w, wpq_bf, subk_bf, *, tm):
    n = x2d.shape[0]
    row = lambda i: (i, 0)
    return pl.pallas_call(
        _finish_kernel,
        grid=(n // tm,),
        in_specs=[
            pl.BlockSpec((tm, D_MODEL), row),
            pl.BlockSpec((tm, GROUP_WIDTH), row),
            pl.BlockSpec((tm, D_MODEL), row),
            pl.BlockSpec((tm, 2 * D_MODEL), row),
            _full((GROUP_WIDTH, D_MODEL)),
            _full((D_MODEL, D_MODEL)),
            _full((1, D_MODEL)),
            _full((D_MODEL, PEER_QCOLS)),
            _full((2 * PEER_HEADS, N_KEYS, PEER_HALF)),
        ],
        out_specs=[
            pl.BlockSpec((tm, D_MODEL), row),
            pl.BlockSpec((tm, D_MODEL), row),
            pl.BlockSpec((PEER_QCOLS, tm), lambda i: (0, i)),
        ],
        out_shape=[
            jax.ShapeDtypeStruct((n, D_MODEL), F32),
            jax.ShapeDtypeStruct((n, D_MODEL), BF16),
            jax.ShapeDtypeStruct((PEER_QCOLS, n), F32),
        ],
        compiler_params=_params("parallel"),
        name="finish",
    )(x2d, heads, conv_out, gates, wap_bf, wout_bf, n2w, wpq_bf, subk_bf)


def _top_values(x, count):
    vals = []
    for _ in range(count):
        m = jnp.max(x, axis=0, keepdims=True)
        vals.append(m)
        x = jnp.where(x == m, NEG_INF, x)
    return vals


def _peer_prep_kernel(st_ref, s1_ref, c1_ref, s2_ref, e2_ref, tau_ref):
    taus = []
    for h in range(PEER_HEADS):
        r1 = slice((2 * h) * N_KEYS, (2 * h + 1) * N_KEYS)
        r2 = slice((2 * h + 1) * N_KEYS, (2 * h + 2) * N_KEYS)
        out_rows = slice(h * N_KEYS, (h + 1) * N_KEYS)
        s1 = st_ref[r1, :]
        s2 = st_ref[r2, :]
        top1 = _top_values(s1, PEER_TOPK)
        top2 = _top_values(s2, PEER_TOPK)
        b16 = jnp.concatenate(top2, axis=0)
        b8 = b16[:8]
        cand = jnp.concatenate([top1[a] + (b16 if a < 2 else b8) for a in range(PEER_TOPK)], axis=0)
        tops = _top_values(cand, PEER_TOPK)
        best, tau = tops[0], tops[-1]
        z = jnp.sum(jnp.where(cand >= tau, jnp.exp(cand - best), 0.0), axis=0, keepdims=True)
        s1_ref[out_rows, :] = s1
        c1_ref[out_rows, :] = jnp.exp(s1 - top1[0]) / z
        s2_ref[out_rows, :] = s2
        e2_ref[out_rows, :] = jnp.exp(s2 - top2[0])
        taus.append(tau)
    tau_ref[...] = jnp.concatenate(taus, axis=0)


def _peer_prep(st, *, tn):
    n = st.shape[1]
    half_rows = PEER_HEADS * N_KEYS
    col = lambda i: (0, i)
    return pl.pallas_call(
        _peer_prep_kernel,
        grid=(n // tn,),
        in_specs=[pl.BlockSpec((PEER_QCOLS, tn), col)],
        out_specs=[pl.BlockSpec((half_rows, tn), col)] * 4 + [pl.BlockSpec((PEER_HEADS, tn), col)],
        out_shape=[jax.ShapeDtypeStruct((half_rows, n), F32)] * 4
        + [jax.ShapeDtypeStruct((PEER_HEADS, n), F32)],
        compiler_params=_params("parallel"),
        name="peer_prep",
    )(st)


def _gelu(a):
    return 0.5 * a * (1.0 + lax.erf(a * np.float32(1.0 / np.sqrt(2.0))))


def _peer_dense_kernel(h2_ref, u_ref, vt_ref, s1_ref, c1_ref, s2_ref, e2_ref, tau_ref, x1_ref,
                       y_ref, acc_ref, *, keys_per_step):
    j = pl.program_id(1)

    @pl.when(j == 0)
    def _():
        acc_ref[...] = jnp.zeros_like(acc_ref)

    a_t = lax.dot_general(u_ref[...], h2_ref[...], (((1,), (1,)), ((), ())), preferred_element_type=F32)
    act = _gelu(a_t)
    parts = []
    for ii in range(keys_per_step):
        i = j * keys_per_step + ii
        w = jnp.zeros((N_KEYS, a_t.shape[1]), F32)
        for h in range(PEER_HEADS):
            rows = slice(h * N_KEYS, (h + 1) * N_KEYS)
            s1 = s1_ref[pl.ds(h * N_KEYS + i, 1), :]
            c1 = c1_ref[pl.ds(h * N_KEYS + i, 1), :]
            chosen = (s2_ref[rows, :] + s1) >= tau_ref[h:h + 1, :]
            w = w + jnp.where(chosen, e2_ref[rows, :] * c1, 0.0)
        parts.append(w)
    weights = jnp.concatenate(parts, axis=0)
    acc_ref[...] += jnp.dot(vt_ref[...], (weights * act).astype(BF16), preferred_element_type=F32)

    @pl.when(j == pl.num_programs(1) - 1)
    def _():
        y_ref[...] = x1_ref[...] + acc_ref[...].T


def _peer_dense(h2, eu_bf, evt_bf, s1, c1, s2, e2, tau, x1, *, tn, keys_per_step):
    n = h2.shape[0]
    ec = keys_per_step * N_KEYS
    half_rows = PEER_HEADS * N_KEYS
    tok = lambda i, j: (i, 0)
    tcol = lambda i, j: (0, i)
    return pl.pallas_call(
        functools.partial(_peer_dense_kernel, keys_per_step=keys_per_step),
        grid=(n // tn, N_EXPERTS // ec),
        in_specs=[
            pl.BlockSpec((tn, D_MODEL), tok),
            pl.BlockSpec((ec, D_MODEL), lambda i, j: (j, 0)),
            pl.BlockSpec((D_MODEL, ec), lambda i, j: (0, j)),
            pl.BlockSpec((half_rows, tn), tcol),
            pl.BlockSpec((half_rows, tn), tcol),
            pl.BlockSpec((half_rows, tn), tcol),
            pl.BlockSpec((half_rows, tn), tcol),
            pl.BlockSpec((PEER_HEADS, tn), tcol),
            pl.BlockSpec((tn, D_MODEL), tok),
        ],
        out_specs=pl.BlockSpec((tn, D_MODEL), tok),
        out_shape=jax.ShapeDtypeStruct((n, D_MODEL), F32),
        scratch_shapes=[pltpu.VMEM((D_MODEL, tn), F32)],
        compiler_params=_params("parallel", "arbitrary"),
        name="peer_dense",
    )(h2, eu_bf, evt_bf, s1, c1, s2, e2, tau, x1)


def _token_tail(x2d, heads, conv_out, gates, lw, *, tm, tn, keys_per_step):
    x1, h2, st = _finish(x2d, heads, conv_out, gates, lw["wap"], lw["wout"], lw["n2"], lw["wpq"],
                         lw["subk"], tm=tm)
    s1, c1, s2, e2, tau = _peer_prep(st, tn=tn)
    return _peer_dense(h2, lw["eu"], lw["evt"], s1, c1, s2, e2, tau, x1, tn=tn, keys_per_step=keys_per_step)


def kernel(x_prompt, x_sample, cache_kv_g1, cache_kv_g2, cache_kv_g3, state_conv,
           norm1_w, w_in, q_norm_w, k_norm_w, conv_w, conv_b, conv_ln_g, conv_ln_b,
           w_conv_proj, w_attn_proj, w_out, norm2_w, w_peer_q, peer_sub_keys,
           expert_u, expert_v):
    depth = w_in.shape[0]
    bsz, seq, _ = x_prompt.shape
    bd, n_new, _ = x_sample.shape
    past_len = cache_kv_g3.shape[2]
    caches = (cache_kv_g1, cache_kv_g2, cache_kv_g3)

    seg = _head_mean_matrix()
    ind = _head_indicator()
    indt = ind.T
    cos_p, sin_p = _rope_tables(np.arange(seq))
    cos_s, sin_s = _rope_tables(past_len + np.arange(bd * n_new) % n_new)
    q_scale = HEAD_DIM ** -0.5

    hp = x_prompt.reshape(bsz * seq, D_MODEL)
    hs = x_sample.reshape(bd * n_new, D_MODEL)
    kv_p = [[], [], []]
    kv_s = [[], [], []]
    conv_p, conv_s = [], []
    for layer in range(depth):
        tile = lambda w: jnp.tile(w[layer], GROUP_HEADS)[None, :]
        lw = dict(
            wap=w_attn_proj[layer].astype(BF16), wout=w_out[layer].astype(BF16),
            n2=norm2_w[layer][None, :], wpq=w_peer_q[layer].astype(BF16),
            subk=peer_sub_keys[layer].reshape(2 * PEER_HEADS, N_KEYS, PEER_HALF).astype(BF16),
            eu=expert_u[layer].astype(BF16), evt=expert_v[layer].T.astype(BF16))
        w_in_bf = w_in[layer].astype(BF16)
        n1 = norm1_w[layer][None, :]
        qnw, knw = tile(q_norm_w), tile(k_norm_w)
        conv_args = (conv_w[layer], conv_b[layer][None, :], conv_ln_g[layer][None, :],
                     conv_ln_b[layer][None, :], w_conv_proj[layer].astype(BF16))

        q, k, v, u, gates = _mixer_in(hp, seq, cos_p, sin_p, n1, w_in_bf, qnw, knw, seg,
                                      tm=512, q_dtype=BF16, q_scale=q_scale)
        u3 = u.reshape(bsz, seq, CONV_DIM)
        c_out = _conv_prompt(u3, *conv_args, tm=512).reshape(bsz * seq, D_MODEL)
        q3, k3, v3 = (a.reshape(bsz, seq, ATTN_WIDTH) for a in (q, k, v))
        outs, lses = [], []
        for g in range(N_GROUPS):
            o, l = _attn_prompt_group(q3, k3, v3, g)
            outs.append(o.reshape(bsz * seq, GROUP_WIDTH))
            lses.append(l.reshape(bsz * seq, GROUP_WIDTH))
        heads = _combine_groups(outs, lses, tm=1024)
        hp = _token_tail(hp, heads, c_out, gates, lw, tm=512, tn=512, keys_per_step=4)
        conv_p.append(u3[:, seq - (CONV_WIDTH - 1):])
        for g in range(N_GROUPS):
            keep = min(GROUP_WINDOWS[g], seq)
            gs = slice(g * GROUP_WIDTH, (g + 1) * GROUP_WIDTH)
            kv = jnp.stack([k3[:, seq - keep:, gs], v3[:, seq - keep:, gs]], axis=2)
            kv_p[g].append(kv.reshape(bsz, keep, 2, GROUP_HEADS, HEAD_DIM))

        q, k, v, u, gates = _mixer_in(hs, bd * n_new, cos_s, sin_s, n1, w_in_bf, qnw, knw, seg,
                                      tm=bd * n_new, q_dtype=F32, q_scale=q_scale)
        u3 = u.reshape(bd, n_new, CONV_DIM)
        xp = jnp.concatenate([state_conv[layer], u3], axis=1)
        c_out = _conv_sample(xp.transpose(1, 0, 2), *conv_args)
        c_out = c_out.transpose(1, 0, 2).reshape(bd * n_new, D_MODEL)
        heads = _attn_sample(q, k, v, *(c[layer] for c in caches), ind, indt, n_new=n_new, bb=4)
        hs = _token_tail(hs, heads, c_out, gates, lw, tm=bd * n_new, tn=bd * n_new, keys_per_step=4)
        conv_s.append(xp[:, n_new:])
        k3, v3 = (a.reshape(bd, n_new, ATTN_WIDTH) for a in (k, v))
        for g in range(N_GROUPS):
            gs = slice(g * GROUP_WIDTH, (g + 1) * GROUP_WIDTH)
            kv = jnp.stack([k3[:, :, gs], v3[:, :, gs]], axis=2)
            kv_s[g].append(kv.reshape(bd, n_new, 2, GROUP_HEADS, HEAD_DIM))

    return (hp.reshape(bsz, seq, D_MODEL), hs.reshape(bd, n_new, D_MODEL),
            jnp.stack(kv_p[0]), jnp.stack(kv_p[1]), jnp.stack(kv_p[2]), jnp.stack(conv_p),
            jnp.stack(kv_s[0]), jnp.stack(kv_s[1]), jnp.stack(kv_s[2]), jnp.stack(conv_s))
```

```python
import functools

import numpy as np
import jax
import jax.numpy as jnp
from jax import lax
from jax.experimental import pallas as pl
from jax.experimental.pallas import tpu as pltpu

F32 = jnp.float32
BF16 = jnp.bfloat16

D_MODEL = 1024
HEAD_DIM = 64
HALF_DIM = HEAD_DIM // 2
GROUP_HEADS = 8
N_GROUPS = 3
GROUP_WIDTH = GROUP_HEADS * HEAD_DIM
ATTN_WIDTH = N_GROUPS * GROUP_WIDTH
GROUP_WINDOWS = (128, 512, 2048)
GROUP_DILATIONS = (1, 4, 16)
WINDOW_STEPS = 128
ROPE_THETA = 10000.0
CONV_DIM = D_MODEL
CONV_WIDTH = 31
CONV_HALO = 32
PEER_HEADS = 8
N_KEYS = 128
N_EXPERTS = N_KEYS * N_KEYS
PEER_TOPK = 16
PEER_HALF = 128
PEER_QCOLS = PEER_HEADS * 2 * PEER_HALF
RMS_EPS = 1e-6
LN_EPS = 1e-5
NEG_INF = float("-inf")

VMEM_LIMIT = 56 * 1024 * 1024


def _params(*sem):
    return pltpu.CompilerParams(dimension_semantics=sem, vmem_limit_bytes=VMEM_LIMIT)


def _full(shape):
    return pl.BlockSpec(shape, lambda *_: (0,) * len(shape), pipeline_mode=pl.Buffered(1))


def _split_bf16(x):
    hi = x.astype(BF16)
    lo = (x - hi.astype(F32)).astype(BF16)
    return hi, lo


def _rope_tables(positions):
    inv_freq = ROPE_THETA ** (-np.arange(HALF_DIM, dtype=np.float64) / HALF_DIM)
    ang = np.asarray(positions, np.float64)[:, None] * inv_freq[None, :]
    cos = np.cos(ang)
    sin = np.sin(ang)
    cos_t = np.concatenate([cos, cos, cos, cos], axis=1)
    sin_t = np.concatenate([-sin, sin, -sin, sin], axis=1)
    return jnp.asarray(cos_t, F32), jnp.asarray(sin_t, F32)


def _head_mean_matrix():
    idx = np.arange(GROUP_WIDTH) // HEAD_DIM
    return jnp.asarray((idx[:, None] == idx[None, :]) / HEAD_DIM, BF16)


def _mixer_in_kernel(x_ref, n1_ref, w_ref, cos_ref, sin_ref, qn_ref, kn_ref, seg_ref,
                     q_ref, k_ref, v_ref, u_ref, g_ref, *, q_scale):
    tm = x_ref.shape[0]
    x = x_ref[...]
    ms = jnp.mean(x * x, axis=-1, keepdims=True)
    h = (x * lax.rsqrt(ms + RMS_EPS) * n1_ref[...]).astype(BF16)
    cos = jnp.concatenate([cos_ref[...]] * 4, axis=1)
    sin = jnp.concatenate([sin_ref[...]] * 4, axis=1)
    lane = lax.broadcasted_iota(jnp.int32, (tm, GROUP_WIDTH), 1)
    first_half = (lane % HEAD_DIM) < HALF_DIM

    def proj(c0, width):
        return jnp.dot(h, w_ref[:, c0:c0 + width], preferred_element_type=F32)

    def normed_rotated(c0, nw_ref):
        z = proj(c0, GROUP_WIDTH)
        msq = jnp.dot((z * z).astype(BF16), seg_ref[...], preferred_element_type=F32)
        zn = z * lax.rsqrt(msq + RMS_EPS) * nw_ref[...]
        swapped = jnp.where(first_half,
                            pltpu.roll(zn, GROUP_WIDTH - HALF_DIM, 1),
                            pltpu.roll(zn, HALF_DIM, 1))
        return zn * cos + swapped * sin

    for g in range(N_GROUPS):
        cs = slice(g * GROUP_WIDTH, (g + 1) * GROUP_WIDTH)
        q_ref[:, cs] = (normed_rotated(g * GROUP_WIDTH, qn_ref) * q_scale).astype(q_ref.dtype)
        k_ref[:, cs] = normed_rotated(ATTN_WIDTH + g * GROUP_WIDTH, kn_ref)
        v_ref[:, cs] = proj(2 * ATTN_WIDTH + g * GROUP_WIDTH, GROUP_WIDTH)
    glu0 = 3 * ATTN_WIDTH
    for c in range(CONV_DIM // GROUP_WIDTH):
        cs = slice(c * GROUP_WIDTH, (c + 1) * GROUP_WIDTH)
        a = proj(glu0 + c * GROUP_WIDTH, GROUP_WIDTH)
        b = proj(glu0 + CONV_DIM + c * GROUP_WIDTH, GROUP_WIDTH)
        u_ref[:, cs] = a * jax.nn.sigmoid(b)
    gate0 = glu0 + 2 * CONV_DIM
    for c in range(2 * D_MODEL // GROUP_WIDTH):
        cs = slice(c * GROUP_WIDTH, (c + 1) * GROUP_WIDTH)
        g_ref[:, cs] = jax.nn.sigmoid(proj(gate0 + c * GROUP_WIDTH, GROUP_WIDTH))


def _mixer_in(x2d, positions_period, cos_t, sin_t, n1w, w_in_bf, qnw, knw, seg, *, tm, q_dtype, q_scale):
    n = x2d.shape[0]
    in_cols = w_in_bf.shape[1]
    period_tiles = positions_period // tm
    row = lambda i: (i, 0)
    return pl.pallas_call(
        functools.partial(_mixer_in_kernel, q_scale=q_scale),
        grid=(n // tm,),
        in_specs=[
            pl.BlockSpec((tm, D_MODEL), row),
            _full((1, D_MODEL)),
            _full((D_MODEL, in_cols)),
            pl.BlockSpec((tm, 128), lambda i: (i % period_tiles, 0)),
            pl.BlockSpec((tm, 128), lambda i: (i % period_tiles, 0)),
            _full((1, GROUP_WIDTH)),
            _full((1, GROUP_WIDTH)),
            _full((GROUP_WIDTH, GROUP_WIDTH)),
        ],
        out_specs=[
            pl.BlockSpec((tm, ATTN_WIDTH), row),
            pl.BlockSpec((tm, ATTN_WIDTH), row),
            pl.BlockSpec((tm, ATTN_WIDTH), row),
            pl.BlockSpec((tm, CONV_DIM), row),
            pl.BlockSpec((tm, 2 * D_MODEL), row),
        ],
        out_shape=[
            jax.ShapeDtypeStruct((n, ATTN_WIDTH), q_dtype),
            jax.ShapeDtypeStruct((n, ATTN_WIDTH), F32),
            jax.ShapeDtypeStruct((n, ATTN_WIDTH), F32),
            jax.ShapeDtypeStruct((n, CONV_DIM), F32),
            jax.ShapeDtypeStruct((n, 2 * D_MODEL), F32),
        ],
        compiler_params=_params("parallel"),
        name="mixer_in",
    )(x2d, n1w, w_in_bf, cos_t, sin_t, qnw, knw, seg)


CONV_ROWS = 32
CONV_LANES = 256


def _conv_tail(y, b_ref, lg_ref, lb_ref, wp_ref):
    y = y + b_ref[...]
    mu = jnp.mean(y, axis=-1, keepdims=True)
    yc = y - mu
    var = jnp.mean(yc * yc, axis=-1, keepdims=True)
    yn = yc * lax.rsqrt(var + LN_EPS) * lg_ref[...] + lb_ref[...]
    act = yn * jax.nn.sigmoid(yn)
    return jnp.dot(act.astype(BF16), wp_ref[...], preferred_element_type=F32)


def _conv_prompt_kernel(u_ref, prev_ref, cw_ref, b_ref, lg_ref, lb_ref, wp_ref, o_ref, xs_ref, y_ref):
    tm = u_ref.shape[0]
    i = pl.program_id(1)
    prev = prev_ref[...]
    xs_ref[0:CONV_HALO, :] = jnp.where(i == 0, jnp.zeros_like(prev), prev)
    xs_ref[CONV_HALO:, :] = u_ref[...]
    first = CONV_HALO - (CONV_WIDTH - 1)

    def rows(r, carry):
        base = pl.multiple_of(r * CONV_ROWS, CONV_ROWS)
        for c in range(CONV_DIM // CONV_LANES):
            cs = slice(c * CONV_LANES, (c + 1) * CONV_LANES)
            window = xs_ref[pl.ds(base, CONV_ROWS + CONV_HALO), cs]
            acc = jnp.zeros((CONV_ROWS, CONV_LANES), F32)
            for w in range(CONV_WIDTH):
                acc = acc + window[first + w:first + w + CONV_ROWS, :] * cw_ref[w:w + 1, cs]
            y_ref[pl.ds(base, CONV_ROWS), cs] = acc
        return carry

    lax.fori_loop(0, tm // CONV_ROWS, rows, 0)
    o_ref[...] = _conv_tail(y_ref[...], b_ref, lg_ref, lb_ref, wp_ref)


def _conv_prompt(u3d, cw, cb, lg, lb, wp_bf, *, tm):
    bsz, t, _ = u3d.shape
    halo_per_tile = tm // CONV_HALO
    return pl.pallas_call(
        _conv_prompt_kernel,
        grid=(bsz, t // tm),
        in_specs=[
            pl.BlockSpec((None, tm, CONV_DIM), lambda b, i: (b, i, 0)),
            pl.BlockSpec((None, CONV_HALO, CONV_DIM),
                         lambda b, i: (b, jnp.maximum(i * halo_per_tile - 1, 0), 0)),
            _full((CONV_WIDTH, CONV_DIM)),
            _full((1, CONV_DIM)), _full((1, CONV_DIM)), _full((1, CONV_DIM)),
            _full((CONV_DIM, D_MODEL)),
        ],
        out_specs=pl.BlockSpec((None, tm, D_MODEL), lambda b, i: (b, i, 0)),
        out_shape=jax.ShapeDtypeStruct((bsz, t, D_MODEL), F32),
        scratch_shapes=[pltpu.VMEM((tm + CONV_HALO, CONV_DIM), F32),
                        pltpu.VMEM((tm, CONV_DIM), F32)],
        compiler_params=_params("parallel", "arbitrary"),
        name="conv_prompt",
    )(u3d, u3d, cw, cb, lg, lb, wp_bf)


def _conv_sample_kernel(xp_ref, cw_ref, b_ref, lg_ref, lb_ref, wp_ref, o_ref):
    n_new = o_ref.shape[0]
    for s in range(n_new):
        acc = jnp.zeros(xp_ref.shape[1:], F32)
        for w in range(CONV_WIDTH):
            acc = acc + xp_ref[s + w] * cw_ref[w:w + 1, :]
        o_ref[s] = _conv_tail(acc, b_ref, lg_ref, lb_ref, wp_ref)


def _conv_sample(xp_t, cw, cb, lg, lb, wp_bf):
    rows, bd, _ = xp_t.shape
    n_new = rows - (CONV_WIDTH - 1)
    return pl.pallas_call(
        _conv_sample_kernel,
        grid=(1,),
        in_specs=[_full(xp_t.shape), _full((CONV_WIDTH, CONV_DIM)),
                  _full((1, CONV_DIM)), _full((1, CONV_DIM)), _full((1, CONV_DIM)),
                  _full((CONV_DIM, D_MODEL))],
        out_specs=pl.BlockSpec((n_new, bd, D_MODEL), lambda i: (0, 0, 0)),
        out_shape=jax.ShapeDtypeStruct((n_new, bd, D_MODEL), F32),
        compiler_params=_params("arbitrary"),
        name="conv_sample",
    )(xp_t, cw, cb, lg, lb, wp_bf)


def _attn_prompt_kernel(q_ref, kp_ref, kc_ref, vp_ref, vc_ref, o_ref, l_ref):
    blk = q_ref.shape[0]
    n = pl.program_id(2)
    qi = lax.broadcasted_iota(jnp.int32, (blk, 2 * blk), 0)
    kj = lax.broadcasted_iota(jnp.int32, (blk, 2 * blk), 1)
    first_key = jnp.where(n > 0, qi, blk)
    valid = (kj >= first_key) & (kj <= qi + blk)
    outs, lses = [], []
    for h in range(GROUP_HEADS):
        cs = slice(h * HEAD_DIM, (h + 1) * HEAD_DIM)
        q = q_ref[:, cs]
        k = jnp.concatenate([kp_ref[:, cs], kc_ref[:, cs]], axis=0).astype(BF16)
        v = jnp.concatenate([vp_ref[:, cs], vc_ref[:, cs]], axis=0).astype(BF16)
        s = lax.dot_general(q, k, (((1,), (1,)), ((), ())), preferred_element_type=F32)
        s = jnp.where(valid, s, NEG_INF)
        m = jnp.max(s, axis=-1, keepdims=True)
        p = jnp.exp(s - m)
        l = jnp.sum(p, axis=-1, keepdims=True)
        o = jnp.dot(p.astype(BF16), v, preferred_element_type=F32) / l
        outs.append(o)
        lses.append(jnp.broadcast_to(m + jnp.log(l), (blk, HEAD_DIM)))
    o_ref[...] = jnp.concatenate(outs, axis=1)
    l_ref[...] = jnp.concatenate(lses, axis=1)


def _attn_prompt_group(q3d, k3d, v3d, g):
    bsz, t, _ = q3d.shape
    d = GROUP_DILATIONS[g]
    m = t // d
    blk = WINDOW_STEPS
    assert m % blk == 0
    nb = m // blk
    view = lambda a: a.reshape(bsz, m, d * ATTN_WIDTH)
    cur = lambda b, r, n: (b, n, r * N_GROUPS + g)
    prv = lambda b, r, n: (b, jnp.maximum(n - 1, 0), r * N_GROUPS + g)
    blk_spec = lambda im: pl.BlockSpec((None, blk, GROUP_WIDTH), im)
    o, lse = pl.pallas_call(
        _attn_prompt_kernel,
        grid=(bsz, d, nb),
        in_specs=[blk_spec(cur), blk_spec(prv), blk_spec(cur), blk_spec(prv), blk_spec(cur)],
        out_specs=[pl.BlockSpec((None, blk, GROUP_WIDTH), lambda b, r, n: (b, n, r))] * 2,
        out_shape=[jax.ShapeDtypeStruct((bsz, m, d * GROUP_WIDTH), F32)] * 2,
        compiler_params=_params("parallel", "parallel", "arbitrary"),
        name=f"attn_prompt_g{g + 1}",
    )(view(q3d), view(k3d), view(k3d), view(v3d), view(v3d))
    return o.reshape(bsz, t, GROUP_WIDTH), lse.reshape(bsz, t, GROUP_WIDTH)


def _combine_kernel(o1, o2, o3, l1, l2, l3, out_ref):
    la, lb, lc = l1[...], l2[...], l3[...]
    mx = jnp.maximum(jnp.maximum(la, lb), lc)
    ea, eb, ec = jnp.exp(la - mx), jnp.exp(lb - mx), jnp.exp(lc - mx)
    out_ref[...] = (ea * o1[...] + eb * o2[...] + ec * o3[...]) / (ea + eb + ec)


def _combine_groups(outs, lses, *, tm):
    n = outs[0].shape[0]
    spec = pl.BlockSpec((tm, GROUP_WIDTH), lambda i: (i, 0))
    return pl.pallas_call(
        _combine_kernel,
        grid=(n // tm,),
        in_specs=[spec] * 6,
        out_specs=spec,
        out_shape=jax.ShapeDtypeStruct((n, GROUP_WIDTH), F32),
        compiler_params=_params("parallel"),
        name="combine_groups",
    )(*outs, *lses)


SAMPLE_ROWS = 8


def _attn_sample_kernel(q_ref, kn_ref, vn_ref, c1_ref, c2_ref, c3_ref, o_ref, *, n_new):
    row = lax.broadcasted_iota(jnp.int32, (GROUP_HEADS, SAMPLE_ROWS, 1), 1)
    stats = []
    for g, c_ref in enumerate((c1_ref, c2_ref, c3_ref)):
        d = GROUP_DILATIONS[g]
        span = c_ref.shape[-1]
        q = q_ref[g]
        k_old = c_ref[0].astype(BF16)
        v_old = c_ref[1].astype(BF16)
        s_old = lax.dot_general(q.astype(BF16), k_old, (((2,), (1,)), ((0,), (0,))),
                                preferred_element_type=F32)
        pos = lax.broadcasted_iota(jnp.int32, (GROUP_HEADS, SAMPLE_ROWS, span), 2)
        ok = (((pos - row) & (d - 1)) == 0) & (pos >= row)
        s_old = jnp.where(ok, s_old, NEG_INF)
        mx = jnp.max(s_old, axis=-1, keepdims=True)
        s_new = []
        for k in range(n_new):
            sk = jnp.sum(q * kn_ref[g, :, k:k + 1, :], axis=-1, keepdims=True)
            ok_new = (row >= k) if g == 0 else (row == k)
            sk = jnp.where(ok_new, sk, NEG_INF)
            s_new.append(sk)
            mx = jnp.maximum(mx, sk)
        p_old = jnp.exp(s_old - mx)
        den = jnp.sum(p_old, axis=-1, keepdims=True)
        acc = lax.dot_general(p_old.astype(BF16), v_old, (((2,), (2,)), ((0,), (0,))),
                              preferred_element_type=F32)
        for k in range(n_new):
            pk = jnp.exp(s_new[k] - mx)
            den = den + pk
            acc = acc + pk * vn_ref[g, :, k:k + 1, :]
        stats.append((mx, den, acc))
    top = jnp.maximum(jnp.maximum(stats[0][0], stats[1][0]), stats[2][0])
    total = jnp.zeros_like(top)
    out = jnp.zeros(o_ref.shape, F32)
    for mx, den, acc in stats:
        w = jnp.exp(mx - top)
        total = total + w * den
        out = out + w * acc
    o_ref[...] = out / total


def _per_head_rows(a2d, bd, n_new):
    a = a2d.reshape(bd, n_new, N_GROUPS, GROUP_HEADS, HEAD_DIM).transpose(0, 2, 3, 1, 4)
    return jnp.pad(a, ((0, 0), (0, 0), (0, 0), (0, SAMPLE_ROWS - n_new), (0, 0)))


def _attn_sample(q2d, k2d, v2d, cache1, cache2, cache3, *, n_new):
    bd = q2d.shape[0] // n_new
    assert n_new <= min(SAMPLE_ROWS, GROUP_DILATIONS[1])
    views = []
    for g, c in enumerate((cache1, cache2, cache3)):
        assert c.shape[1] == GROUP_WINDOWS[g]
        views.append(c.transpose(0, 2, 3, 4, 1))
    small = pl.BlockSpec((None, N_GROUPS, GROUP_HEADS, SAMPLE_ROWS, HEAD_DIM), lambda i: (i, 0, 0, 0, 0))
    cspec = lambda span: pl.BlockSpec((None, 2, GROUP_HEADS, HEAD_DIM, span), lambda i: (i, 0, 0, 0, 0))
    out = pl.pallas_call(
        functools.partial(_attn_sample_kernel, n_new=n_new),
        grid=(bd,),
        in_specs=[small, small, small] + [cspec(w) for w in GROUP_WINDOWS],
        out_specs=pl.BlockSpec((None, GROUP_HEADS, SAMPLE_ROWS, HEAD_DIM), lambda i: (i, 0, 0, 0)),
        out_shape=jax.ShapeDtypeStruct((bd, GROUP_HEADS, SAMPLE_ROWS, HEAD_DIM), F32),
        compiler_params=_params("parallel"),
        name="attn_sample",
    )(*(_per_head_rows(a, bd, n_new) for a in (q2d, k2d, v2d)), *views)
    return out[:, :, :n_new].transpose(0, 2, 1, 3).reshape(bd * n_new, GROUP_WIDTH)


def _finish_kernel(x_ref, heads_ref, conv_ref, gate_ref, wap_ref, wout_ref, n2_ref, wpq_ref, sk_ref,
                   x1_ref, h2_ref, st_ref):
    attn = jnp.dot(heads_ref[...].astype(BF16), wap_ref[...], preferred_element_type=F32)
    mixed = gate_ref[:, :D_MODEL] * conv_ref[...] + gate_ref[:, D_MODEL:] * attn
    x1 = x_ref[...] + jnp.dot(mixed.astype(BF16), wout_ref[...], preferred_element_type=F32)
    x1_ref[...] = x1
    ms = jnp.mean(x1 * x1, axis=-1, keepdims=True)
    h2 = (x1 * lax.rsqrt(ms + RMS_EPS) * n2_ref[...]).astype(BF16)
    h2_ref[...] = h2
    qp = jnp.dot(h2, wpq_ref[...], preferred_element_type=F32).astype(BF16)
    for c in range(PEER_QCOLS // PEER_HALF):
        cs = slice(c * PEER_HALF, (c + 1) * PEER_HALF)
        st_ref[cs, :] = lax.dot_general(sk_ref[c], qp[:, cs], (((1,), (1,)), ((), ())),
                                        preferred_element_type=F32)


def _finish(x2d, heads, conv_out, gates, wap_bf, wout_bf, n2w, wpq_bf, subk_bf, *, tm):
    n = x2d.shape[0]
    row = lambda i: (i, 0)
    return pl.pallas_call(
        _finish_kernel,
        grid=(n // tm,),
        in_specs=[
            pl.BlockSpec((tm, D_MODEL), row),
            pl.BlockSpec((tm, GROUP_WIDTH), row),
            pl.BlockSpec((tm, D_MODEL), row),
            pl.BlockSpec((tm, 2 * D_MODEL), row),
            _full((GROUP_WIDTH, D_MODEL)),
            _full((D_MODEL, D_MODEL)),
            _full((1, D_MODEL)),
            _full((D_MODEL, PEER_QCOLS)),
            _full((2 * PEER_HEADS, N_KEYS, PEER_HALF)),
        ],
        out_specs=[
            pl.BlockSpec((tm, D_MODEL), row),
            pl.BlockSpec((tm, D_MODEL), row),
            pl.BlockSpec((PEER_QCOLS, tm), lambda i: (0, i)),
        ],
        out_shape=[
            jax.ShapeDtypeStruct((n, D_MODEL), F32),
            jax.ShapeDtypeStruct((n, D_MODEL), BF16),
            jax.ShapeDtypeStruct((PEER_QCOLS, n), F32),
        ],
        compiler_params=_params("parallel"),
        name="finish",
    )(x2d, heads, conv_out, gates, wap_bf, wout_bf, n2w, wpq_bf, subk_bf)


def _top_values(x, count):
    vals = []
    for _ in range(count):
        m = jnp.max(x, axis=0, keepdims=True)
        vals.append(m)
        x = jnp.where(x == m, NEG_INF, x)
    return vals


def _peer_prep_kernel(st_ref, s1_ref, c1_ref, s2_ref, e2_ref, tau_ref):
    taus = []
    for h in range(PEER_HEADS):
        r1 = slice((2 * h) * N_KEYS, (2 * h + 1) * N_KEYS)
        r2 = slice((2 * h + 1) * N_KEYS, (2 * h + 2) * N_KEYS)
        out_rows = slice(h * N_KEYS, (h + 1) * N_KEYS)
        s1 = st_ref[r1, :]
        s2 = st_ref[r2, :]
        top1 = _top_values(s1, PEER_TOPK)
        top2 = _top_values(s2, PEER_TOPK)
        b16 = jnp.concatenate(top2, axis=0)
        b8 = b16[:8]
        cand = jnp.concatenate([top1[a] + (b16 if a < 2 else b8) for a in range(PEER_TOPK)], axis=0)
        tops = _top_values(cand, PEER_TOPK)
        best, tau = tops[0], tops[-1]
        z = jnp.sum(jnp.where(cand >= tau, jnp.exp(cand - best), 0.0), axis=0, keepdims=True)
        s1_ref[out_rows, :] = s1
        c1_ref[out_rows, :] = jnp.exp(s1 - top1[0]) / z
        s2_ref[out_rows, :] = s2
        e2_ref[out_rows, :] = jnp.exp(s2 - top2[0])
        taus.append(tau)
    tau_ref[...] = jnp.concatenate(taus, axis=0)


def _peer_prep(st, *, tn):
    n = st.shape[1]
    half_rows = PEER_HEADS * N_KEYS
    col = lambda i: (0, i)
    return pl.pallas_call(
        _peer_prep_kernel,
        grid=(n // tn,),
        in_specs=[pl.BlockSpec((PEER_QCOLS, tn), col)],
        out_specs=[pl.BlockSpec((half_rows, tn), col)] * 4 + [pl.BlockSpec((PEER_HEADS, tn), col)],
        out_shape=[jax.ShapeDtypeStruct((half_rows, n), F32)] * 4
        + [jax.ShapeDtypeStruct((PEER_HEADS, n), F32)],
        compiler_params=_params("parallel"),
        name="peer_prep",
    )(st)


def _gelu(a):
    return 0.5 * a * (1.0 + lax.erf(a * np.float32(1.0 / np.sqrt(2.0))))


def _peer_dense_kernel(h2_ref, u_ref, vt_ref, s1_ref, c1_ref, s2_ref, e2_ref, tau_ref, x1_ref,
                       y_ref, acc_ref, *, keys_per_step):
    j = pl.program_id(1)

    @pl.when(j == 0)
    def _():
        acc_ref[...] = jnp.zeros_like(acc_ref)

    a_t = lax.dot_general(u_ref[...], h2_ref[...], (((1,), (1,)), ((), ())), preferred_element_type=F32)
    act = _gelu(a_t)
    parts = []
    for ii in range(keys_per_step):
        i = j * keys_per_step + ii
        w = jnp.zeros((N_KEYS, a_t.shape[1]), F32)
        for h in range(PEER_HEADS):
            rows = slice(h * N_KEYS, (h + 1) * N_KEYS)
            s1 = s1_ref[pl.ds(h * N_KEYS + i, 1), :]
            c1 = c1_ref[pl.ds(h * N_KEYS + i, 1), :]
            chosen = (s2_ref[rows, :] + s1) >= tau_ref[h:h + 1, :]
            w = w + jnp.where(chosen, e2_ref[rows, :] * c1, 0.0)
        parts.append(w)
    weights = jnp.concatenate(parts, axis=0)
    acc_ref[...] += jnp.dot(vt_ref[...], (weights * act).astype(BF16), preferred_element_type=F32)

    @pl.when(j == pl.num_programs(1) - 1)
    def _():
        y_ref[...] = x1_ref[...] + acc_ref[...].T


def _peer_dense(h2, eu_bf, evt_bf, s1, c1, s2, e2, tau, x1, *, tn, keys_per_step):
    n = h2.shape[0]
    ec = keys_per_step * N_KEYS
    half_rows = PEER_HEADS * N_KEYS
    tok = lambda i, j: (i, 0)
    tcol = lambda i, j: (0, i)
    return pl.pallas_call(
        functools.partial(_peer_dense_kernel, keys_per_step=keys_per_step),
        grid=(n // tn, N_EXPERTS // ec),
        in_specs=[
            pl.BlockSpec((tn, D_MODEL), tok),
            pl.BlockSpec((ec, D_MODEL), lambda i, j: (j, 0)),
            pl.BlockSpec((D_MODEL, ec), lambda i, j: (0, j)),
            pl.BlockSpec((half_rows, tn), tcol),
            pl.BlockSpec((half_rows, tn), tcol),
            pl.BlockSpec((half_rows, tn), tcol),
            pl.BlockSpec((half_rows, tn), tcol),
            pl.BlockSpec((PEER_HEADS, tn), tcol),
            pl.BlockSpec((tn, D_MODEL), tok),
        ],
        out_specs=pl.BlockSpec((tn, D_MODEL), tok),
        out_shape=jax.ShapeDtypeStruct((n, D_MODEL), F32),
        scratch_shapes=[pltpu.VMEM((D_MODEL, tn), F32)],
        compiler_params=_params("parallel", "arbitrary"),
        name="peer_dense",
    )(h2, eu_bf, evt_bf, s1, c1, s2, e2, tau, x1)


def _token_tail(x2d, heads, conv_out, gates, lw, *, tm, tn, keys_per_step):
    x1, h2, st = _finish(x2d, heads, conv_out, gates, lw["wap"], lw["wout"], lw["n2"], lw["wpq"],
                         lw["subk"], tm=tm)
    s1, c1, s2, e2, tau = _peer_prep(st, tn=tn)
    return _peer_dense(h2, lw["eu"], lw["evt"], s1, c1, s2, e2, tau, x1, tn=tn, keys_per_step=keys_per_step)


def kernel(x_prompt, x_sample, cache_kv_g1, cache_kv_g2, cache_kv_g3, state_conv,
           norm1_w, w_in, q_norm_w, k_norm_w, conv_w, conv_b, conv_ln_g, conv_ln_b,
           w_conv_proj, w_attn_proj, w_out, norm2_w, w_peer_q, peer_sub_keys,
           expert_u, expert_v):
    depth = w_in.shape[0]
    bsz, seq, _ = x_prompt.shape
    bd, n_new, _ = x_sample.shape
    past_len = cache_kv_g3.shape[2]
    caches = (cache_kv_g1, cache_kv_g2, cache_kv_g3)

    seg = _head_mean_matrix()
    cos_p, sin_p = _rope_tables(np.arange(seq))
    cos_s, sin_s = _rope_tables(past_len + np.arange(bd * n_new) % n_new)
    q_scale = HEAD_DIM ** -0.5

    hp = x_prompt.reshape(bsz * seq, D_MODEL)
    hs = x_sample.reshape(bd * n_new, D_MODEL)
    kv_p = [[], [], []]
    kv_s = [[], [], []]
    conv_p, conv_s = [], []
    for layer in range(depth):
        tile = lambda w: jnp.tile(w[layer], GROUP_HEADS)[None, :]
        lw = dict(
            wap=w_attn_proj[layer].astype(BF16), wout=w_out[layer].astype(BF16),
            n2=norm2_w[layer][None, :], wpq=w_peer_q[layer].astype(BF16),
            subk=peer_sub_keys[layer].reshape(2 * PEER_HEADS, N_KEYS, PEER_HALF).astype(BF16),
            eu=expert_u[layer].astype(BF16), evt=expert_v[layer].T.astype(BF16))
        w_in_bf = w_in[layer].astype(BF16)
        n1 = norm1_w[layer][None, :]
        qnw, knw = tile(q_norm_w), tile(k_norm_w)
        conv_args = (conv_w[layer], conv_b[layer][None, :], conv_ln_g[layer][None, :],
                     conv_ln_b[layer][None, :], w_conv_proj[layer].astype(BF16))

        q, k, v, u, gates = _mixer_in(hp, seq, cos_p, sin_p, n1, w_in_bf, qnw, knw, seg,
                                      tm=512, q_dtype=BF16, q_scale=q_scale)
        u3 = u.reshape(bsz, seq, CONV_DIM)
        c_out = _conv_prompt(u3, *conv_args, tm=512).reshape(bsz * seq, D_MODEL)
        q3, k3, v3 = (a.reshape(bsz, seq, ATTN_WIDTH) for a in (q, k, v))
        outs, lses = [], []
        for g in range(N_GROUPS):
            o, l = _attn_prompt_group(q3, k3, v3, g)
            outs.append(o.reshape(bsz * seq, GROUP_WIDTH))
            lses.append(l.reshape(bsz * seq, GROUP_WIDTH))
        heads = _combine_groups(outs, lses, tm=1024)
        hp = _token_tail(hp, heads, c_out, gates, lw, tm=512, tn=512, keys_per_step=4)
        conv_p.append(u3[:, seq - (CONV_WIDTH - 1):])
        for g in range(N_GROUPS):
            keep = min(GROUP_WINDOWS[g], seq)
            gs = slice(g * GROUP_WIDTH, (g + 1) * GROUP_WIDTH)
            kv = jnp.stack([k3[:, seq - keep:, gs], v3[:, seq - keep:, gs]], axis=2)
            kv_p[g].append(kv.reshape(bsz, keep, 2, GROUP_HEADS, HEAD_DIM))

        q, k, v, u, gates = _mixer_in(hs, bd * n_new, cos_s, sin_s, n1, w_in_bf, qnw, knw, seg,
                                      tm=bd * n_new, q_dtype=F32, q_scale=q_scale)
        u3 = u.reshape(bd, n_new, CONV_DIM)
        xp = jnp.concatenate([state_conv[layer], u3], axis=1)
        c_out = _conv_sample(xp.transpose(1, 0, 2), *conv_args)
        c_out = c_out.transpose(1, 0, 2).reshape(bd * n_new, D_MODEL)
        heads = _attn_sample(q, k, v, *(c[layer] for c in caches), n_new=n_new)
        hs = _token_tail(hs, heads, c_out, gates, lw, tm=bd * n_new, tn=bd * n_new, keys_per_step=4)
        conv_s.append(xp[:, n_new:])
        k3, v3 = (a.reshape(bd, n_new, ATTN_WIDTH) for a in (k, v))
        for g in range(N_GROUPS):
            gs = slice(g * GROUP_WIDTH, (g + 1) * GROUP_WIDTH)
            kv = jnp.stack([k3[:, :, gs], v3[:, :, gs]], axis=2)
            kv_s[g].append(kv.reshape(bd, n_new, 2, GROUP_HEADS, HEAD_DIM))

    return (hp.reshape(bsz, seq, D_MODEL), hs.reshape(bd, n_new, D_MODEL),
            jnp.stack(kv_p[0]), jnp.stack(kv_p[1]), jnp.stack(kv_p[2]), jnp.stack(conv_p),
            jnp.stack(kv_s[0]), jnp.stack(kv_s[1]), jnp.stack(kv_s[2]), jnp.stack(conv_s))
```

```python
import functools

import numpy as np
import jax
import jax.numpy as jnp
from jax import lax
from jax.experimental import pallas as pl
from jax.experimental.pallas import tpu as pltpu

F32 = jnp.float32
BF16 = jnp.bfloat16

D_MODEL = 1024
HEAD_DIM = 64
HALF_DIM = HEAD_DIM // 2
GROUP_HEADS = 8
N_GROUPS = 3
GROUP_WIDTH = GROUP_HEADS * HEAD_DIM
ATTN_WIDTH = N_GROUPS * GROUP_WIDTH
GROUP_WINDOWS = (128, 512, 2048)
GROUP_DILATIONS = (1, 4, 16)
WINDOW_STEPS = 128
ROPE_THETA = 10000.0
CONV_DIM = D_MODEL
CONV_WIDTH = 31
CONV_HALO = 32
PEER_HEADS = 8
N_KEYS = 128
N_EXPERTS = N_KEYS * N_KEYS
PEER_TOPK = 16
PEER_HALF = 128
PEER_QCOLS = PEER_HEADS * 2 * PEER_HALF
PEER_ROWS = PEER_HEADS * N_KEYS
LANES = 128
SUBLANES = 8
RMS_EPS = 1e-6
LN_EPS = 1e-5
NEG_INF = float("-inf")

VMEM_LIMIT = 56 * 1024 * 1024


def _params(*sem):
    return pltpu.CompilerParams(dimension_semantics=sem, vmem_limit_bytes=VMEM_LIMIT)


def _full(shape):
    return pl.BlockSpec(shape, lambda *_: (0,) * len(shape), pipeline_mode=pl.Buffered(1))


def _split_bf16(x):
    hi = x.astype(BF16)
    lo = (x - hi.astype(F32)).astype(BF16)
    return hi, lo


def _rope_tables(positions):
    inv_freq = ROPE_THETA ** (-np.arange(HALF_DIM, dtype=np.float64) / HALF_DIM)
    ang = np.asarray(positions, np.float64)[:, None] * inv_freq[None, :]
    cos = np.cos(ang)
    sin = np.sin(ang)
    cos_t = np.concatenate([cos, cos, cos, cos], axis=1)
    sin_t = np.concatenate([-sin, sin, -sin, sin], axis=1)
    return jnp.asarray(cos_t, F32), jnp.asarray(sin_t, F32)


def _head_mean_matrix():
    idx = np.arange(GROUP_WIDTH) // HEAD_DIM
    return jnp.asarray((idx[:, None] == idx[None, :]) / HEAD_DIM, BF16)


def _mixer_in_kernel(x_ref, n1_ref, w_ref, cos_ref, sin_ref, qn_ref, kn_ref, seg_ref,
                     q_ref, k_ref, v_ref, u_ref, g_ref, *, q_scale):
    tm = x_ref.shape[0]
    x = x_ref[...]
    ms = jnp.mean(x * x, axis=-1, keepdims=True)
    h = (x * lax.rsqrt(ms + RMS_EPS) * n1_ref[...]).astype(BF16)
    cos = jnp.concatenate([cos_ref[...]] * 4, axis=1)
    sin = jnp.concatenate([sin_ref[...]] * 4, axis=1)
    lane = lax.broadcasted_iota(jnp.int32, (tm, GROUP_WIDTH), 1)
    first_half = (lane % HEAD_DIM) < HALF_DIM

    def proj(c0, width):
        return jnp.dot(h, w_ref[:, c0:c0 + width], preferred_element_type=F32)

    def normed_rotated(c0, nw_ref):
        z = proj(c0, GROUP_WIDTH)
        msq = jnp.dot((z * z).astype(BF16), seg_ref[...], preferred_element_type=F32)
        zn = z * lax.rsqrt(msq + RMS_EPS) * nw_ref[...]
        swapped = jnp.where(first_half,
                            pltpu.roll(zn, GROUP_WIDTH - HALF_DIM, 1),
                            pltpu.roll(zn, HALF_DIM, 1))
        return zn * cos + swapped * sin

    for g in range(N_GROUPS):
        cs = slice(g * GROUP_WIDTH, (g + 1) * GROUP_WIDTH)
        q_ref[:, cs] = (normed_rotated(g * GROUP_WIDTH, qn_ref) * q_scale).astype(q_ref.dtype)
        k_ref[:, cs] = normed_rotated(ATTN_WIDTH + g * GROUP_WIDTH, kn_ref)
        v_ref[:, cs] = proj(2 * ATTN_WIDTH + g * GROUP_WIDTH, GROUP_WIDTH)
    glu0 = 3 * ATTN_WIDTH
    for c in range(CONV_DIM // GROUP_WIDTH):
        cs = slice(c * GROUP_WIDTH, (c + 1) * GROUP_WIDTH)
        a = proj(glu0 + c * GROUP_WIDTH, GROUP_WIDTH)
        b = proj(glu0 + CONV_DIM + c * GROUP_WIDTH, GROUP_WIDTH)
        u_ref[:, cs] = a * jax.nn.sigmoid(b)
    gate0 = glu0 + 2 * CONV_DIM
    for c in range(2 * D_MODEL // GROUP_WIDTH):
        cs = slice(c * GROUP_WIDTH, (c + 1) * GROUP_WIDTH)
        g_ref[:, cs] = jax.nn.sigmoid(proj(gate0 + c * GROUP_WIDTH, GROUP_WIDTH))


def _mixer_in(x2d, positions_period, cos_t, sin_t, n1w, w_in_bf, qnw, knw, seg, *, tm, q_dtype, q_scale):
    n = x2d.shape[0]
    in_cols = w_in_bf.shape[1]
    period_tiles = positions_period // tm
    row = lambda i: (i, 0)
    return pl.pallas_call(
        functools.partial(_mixer_in_kernel, q_scale=q_scale),
        grid=(n // tm,),
        in_specs=[
            pl.BlockSpec((tm, D_MODEL), row),
            _full((1, D_MODEL)),
            _full((D_MODEL, in_cols)),
            pl.BlockSpec((tm, 128), lambda i: (i % period_tiles, 0)),
            pl.BlockSpec((tm, 128), lambda i: (i % period_tiles, 0)),
            _full((1, GROUP_WIDTH)),
            _full((1, GROUP_WIDTH)),
            _full((GROUP_WIDTH, GROUP_WIDTH)),
        ],
        out_specs=[
            pl.BlockSpec((tm, ATTN_WIDTH), row),
            pl.BlockSpec((tm, ATTN_WIDTH), row),
            pl.BlockSpec((tm, ATTN_WIDTH), row),
            pl.BlockSpec((tm, CONV_DIM), row),
            pl.BlockSpec((tm, 2 * D_MODEL), row),
        ],
        out_shape=[
            jax.ShapeDtypeStruct((n, ATTN_WIDTH), q_dtype),
            jax.ShapeDtypeStruct((n, ATTN_WIDTH), F32),
            jax.ShapeDtypeStruct((n, ATTN_WIDTH), F32),
            jax.ShapeDtypeStruct((n, CONV_DIM), F32),
            jax.ShapeDtypeStruct((n, 2 * D_MODEL), F32),
        ],
        compiler_params=_params("parallel"),
        name="mixer_in",
    )(x2d, n1w, w_in_bf, cos_t, sin_t, qnw, knw, seg)


CONV_ROWS = 32
CONV_LANES = 256


def _conv_tail(y, b_ref, lg_ref, lb_ref, wp_ref):
    y = y + b_ref[...]
    mu = jnp.mean(y, axis=-1, keepdims=True)
    yc = y - mu
    var = jnp.mean(yc * yc, axis=-1, keepdims=True)
    yn = yc * lax.rsqrt(var + LN_EPS) * lg_ref[...] + lb_ref[...]
    act = yn * jax.nn.sigmoid(yn)
    return jnp.dot(act.astype(BF16), wp_ref[...], preferred_element_type=F32)


def _conv_prompt_kernel(u_ref, prev_ref, cw_ref, b_ref, lg_ref, lb_ref, wp_ref, o_ref, xs_ref, y_ref):
    tm = u_ref.shape[0]
    i = pl.program_id(1)
    prev = prev_ref[...]
    xs_ref[0:CONV_HALO, :] = jnp.where(i == 0, jnp.zeros_like(prev), prev)
    xs_ref[CONV_HALO:, :] = u_ref[...]
    first = CONV_HALO - (CONV_WIDTH - 1)

    def rows(r, carry):
        base = pl.multiple_of(r * CONV_ROWS, CONV_ROWS)
        for c in range(CONV_DIM // CONV_LANES):
            cs = slice(c * CONV_LANES, (c + 1) * CONV_LANES)
            window = xs_ref[pl.ds(base, CONV_ROWS + CONV_HALO), cs]
            acc = jnp.zeros((CONV_ROWS, CONV_LANES), F32)
            for r in range(SUBLANES):
                taps = [w for w in range(CONV_WIDTH) if (first + w) % SUBLANES == r]
                shifted = window if r == 0 else pltpu.roll(window, CONV_ROWS + CONV_HALO - r, 0)
                for w in taps:
                    q = (first + w) // SUBLANES * SUBLANES
                    acc = acc + shifted[q:q + CONV_ROWS, :] * cw_ref[w:w + 1, cs]
            y_ref[pl.ds(base, CONV_ROWS), cs] = acc
        return carry

    lax.fori_loop(0, tm // CONV_ROWS, rows, 0)
    o_ref[...] = _conv_tail(y_ref[...], b_ref, lg_ref, lb_ref, wp_ref)


def _conv_prompt(u3d, cw, cb, lg, lb, wp_bf, *, tm):
    bsz, t, _ = u3d.shape
    halo_per_tile = tm // CONV_HALO
    return pl.pallas_call(
        _conv_prompt_kernel,
        grid=(bsz, t // tm),
        in_specs=[
            pl.BlockSpec((None, tm, CONV_DIM), lambda b, i: (b, i, 0)),
            pl.BlockSpec((None, CONV_HALO, CONV_DIM),
                         lambda b, i: (b, jnp.maximum(i * halo_per_tile - 1, 0), 0)),
            _full((CONV_WIDTH, CONV_DIM)),
            _full((1, CONV_DIM)), _full((1, CONV_DIM)), _full((1, CONV_DIM)),
            _full((CONV_DIM, D_MODEL)),
        ],
        out_specs=pl.BlockSpec((None, tm, D_MODEL), lambda b, i: (b, i, 0)),
        out_shape=jax.ShapeDtypeStruct((bsz, t, D_MODEL), F32),
        scratch_shapes=[pltpu.VMEM((tm + CONV_HALO, CONV_DIM), F32),
                        pltpu.VMEM((tm, CONV_DIM), F32)],
        compiler_params=_params("parallel", "arbitrary"),
        name="conv_prompt",
    )(u3d, u3d, cw, cb, lg, lb, wp_bf)


def _conv_sample_kernel(xp_ref, cw_ref, b_ref, lg_ref, lb_ref, wp_ref, o_ref):
    n_new = o_ref.shape[0]
    for s in range(n_new):
        acc = jnp.zeros(xp_ref.shape[1:], F32)
        for w in range(CONV_WIDTH):
            acc = acc + xp_ref[s + w] * cw_ref[w:w + 1, :]
        o_ref[s] = _conv_tail(acc, b_ref, lg_ref, lb_ref, wp_ref)


def _conv_sample(xp_t, cw, cb, lg, lb, wp_bf):
    rows, bd, _ = xp_t.shape
    n_new = rows - (CONV_WIDTH - 1)
    return pl.pallas_call(
        _conv_sample_kernel,
        grid=(1,),
        in_specs=[_full(xp_t.shape), _full((CONV_WIDTH, CONV_DIM)),
                  _full((1, CONV_DIM)), _full((1, CONV_DIM)), _full((1, CONV_DIM)),
                  _full((CONV_DIM, D_MODEL))],
        out_specs=pl.BlockSpec((n_new, bd, D_MODEL), lambda i: (0, 0, 0)),
        out_shape=jax.ShapeDtypeStruct((n_new, bd, D_MODEL), F32),
        compiler_params=_params("arbitrary"),
        name="conv_sample",
    )(xp_t, cw, cb, lg, lb, wp_bf)


def _attn_prompt_kernel(q_ref, kp_ref, kc_ref, vp_ref, vc_ref, o_ref, l_ref):
    blk = q_ref.shape[0]
    n = pl.program_id(2)
    qi = lax.broadcasted_iota(jnp.int32, (blk, 2 * blk), 0)
    kj = lax.broadcasted_iota(jnp.int32, (blk, 2 * blk), 1)
    first_key = jnp.where(n > 0, qi, blk)
    valid = (kj >= first_key) & (kj <= qi + blk)
    outs, lses = [], []
    for h in range(GROUP_HEADS):
        cs = slice(h * HEAD_DIM, (h + 1) * HEAD_DIM)
        q = q_ref[:, cs]
        k = jnp.concatenate([kp_ref[:, cs], kc_ref[:, cs]], axis=0).astype(BF16)
        v = jnp.concatenate([vp_ref[:, cs], vc_ref[:, cs]], axis=0).astype(BF16)
        s = lax.dot_general(q, k, (((1,), (1,)), ((), ())), preferred_element_type=F32)
        s = jnp.where(valid, s, NEG_INF)
        m = jnp.max(s, axis=-1, keepdims=True)
        p = jnp.exp(s - m)
        l = jnp.sum(p, axis=-1, keepdims=True)
        o = jnp.dot(p.astype(BF16), v, preferred_element_type=F32) / l
        outs.append(o)
        lses.append(jnp.broadcast_to(m + jnp.log(l), (blk, HEAD_DIM)))
    o_ref[...] = jnp.concatenate(outs, axis=1)
    l_ref[...] = jnp.concatenate(lses, axis=1)


def _attn_prompt_group(q3d, k3d, v3d, g):
    bsz, t, _ = q3d.shape
    d = GROUP_DILATIONS[g]
    m = t // d
    blk = WINDOW_STEPS
    assert m % blk == 0
    nb = m // blk
    view = lambda a: a.reshape(bsz, m, d * ATTN_WIDTH)
    cur = lambda b, r, n: (b, n, r * N_GROUPS + g)
    prv = lambda b, r, n: (b, jnp.maximum(n - 1, 0), r * N_GROUPS + g)
    blk_spec = lambda im: pl.BlockSpec((None, blk, GROUP_WIDTH), im)
    o, lse = pl.pallas_call(
        _attn_prompt_kernel,
        grid=(bsz, d, nb),
        in_specs=[blk_spec(cur), blk_spec(prv), blk_spec(cur), blk_spec(prv), blk_spec(cur)],
        out_specs=[pl.BlockSpec((None, blk, GROUP_WIDTH), lambda b, r, n: (b, n, r))] * 2,
        out_shape=[jax.ShapeDtypeStruct((bsz, m, d * GROUP_WIDTH), F32)] * 2,
        compiler_params=_params("parallel", "parallel", "arbitrary"),
        name=f"attn_prompt_g{g + 1}",
    )(view(q3d), view(k3d), view(k3d), view(v3d), view(v3d))
    return o.reshape(bsz, t, GROUP_WIDTH), lse.reshape(bsz, t, GROUP_WIDTH)


def _combine_kernel(o1, o2, o3, l1, l2, l3, out_ref):
    la, lb, lc = l1[...], l2[...], l3[...]
    mx = jnp.maximum(jnp.maximum(la, lb), lc)
    ea, eb, ec = jnp.exp(la - mx), jnp.exp(lb - mx), jnp.exp(lc - mx)
    out_ref[...] = (ea * o1[...] + eb * o2[...] + ec * o3[...]) / (ea + eb + ec)


def _combine_groups(outs, lses, *, tm):
    n = outs[0].shape[0]
    spec = pl.BlockSpec((tm, GROUP_WIDTH), lambda i: (i, 0))
    return pl.pallas_call(
        _combine_kernel,
        grid=(n // tm,),
        in_specs=[spec] * 6,
        out_specs=spec,
        out_shape=jax.ShapeDtypeStruct((n, GROUP_WIDTH), F32),
        compiler_params=_params("parallel"),
        name="combine_groups",
    )(*outs, *lses)


SAMPLE_ROWS = 8


def _attn_sample_kernel(q_ref, kn_ref, vn_ref, c1_ref, c2_ref, c3_ref, o_ref, *, n_new):
    row = lax.broadcasted_iota(jnp.int32, (GROUP_HEADS, SAMPLE_ROWS, 1), 1)
    stats = []
    for g, c_ref in enumerate((c1_ref, c2_ref, c3_ref)):
        d = GROUP_DILATIONS[g]
        span = c_ref.shape[-1]
        q = q_ref[g]
        k_old = c_ref[0].astype(BF16)
        v_old = c_ref[1].astype(BF16)
        s_old = lax.dot_general(q.astype(BF16), k_old, (((2,), (1,)), ((0,), (0,))),
                                preferred_element_type=F32)
        pos = lax.broadcasted_iota(jnp.int32, (GROUP_HEADS, SAMPLE_ROWS, span), 2)
        ok = (((pos - row) & (d - 1)) == 0) & (pos >= row)
        s_old = jnp.where(ok, s_old, NEG_INF)
        mx = jnp.max(s_old, axis=-1, keepdims=True)
        s_new = []
        for k in range(n_new):
            sk = jnp.sum(q * kn_ref[g, :, k:k + 1, :], axis=-1, keepdims=True)
            ok_new = (row >= k) if g == 0 else (row == k)
            sk = jnp.where(ok_new, sk, NEG_INF)
            s_new.append(sk)
            mx = jnp.maximum(mx, sk)
        p_old = jnp.exp(s_old - mx)
        den = jnp.sum(p_old, axis=-1, keepdims=True)
        acc = lax.dot_general(p_old.astype(BF16), v_old, (((2,), (2,)), ((0,), (0,))),
                              preferred_element_type=F32)
        for k in range(n_new):
            pk = jnp.exp(s_new[k] - mx)
            den = den + pk
            acc = acc + pk * vn_ref[g, :, k:k + 1, :]
        stats.append((mx, den, acc))
    top = jnp.maximum(jnp.maximum(stats[0][0], stats[1][0]), stats[2][0])
    total = jnp.zeros_like(top)
    out = jnp.zeros(o_ref.shape, F32)
    for mx, den, acc in stats:
        w = jnp.exp(mx - top)
        total = total + w * den
        out = out + w * acc
    o_ref[...] = out / total


def _per_head_rows(a2d, bd, n_new):
    a = a2d.reshape(bd, n_new, N_GROUPS, GROUP_HEADS, HEAD_DIM).transpose(0, 2, 3, 1, 4)
    return jnp.pad(a, ((0, 0), (0, 0), (0, 0), (0, SAMPLE_ROWS - n_new), (0, 0)))


def _attn_sample(q2d, k2d, v2d, cache1, cache2, cache3, *, n_new):
    bd = q2d.shape[0] // n_new
    assert n_new <= min(SAMPLE_ROWS, GROUP_DILATIONS[1])
    views = []
    for g, c in enumerate((cache1, cache2, cache3)):
        assert c.shape[1] == GROUP_WINDOWS[g]
        views.append(c.transpose(0, 2, 3, 4, 1))
    small = pl.BlockSpec((None, N_GROUPS, GROUP_HEADS, SAMPLE_ROWS, HEAD_DIM), lambda i: (i, 0, 0, 0, 0))
    cspec = lambda span: pl.BlockSpec((None, 2, GROUP_HEADS, HEAD_DIM, span), lambda i: (i, 0, 0, 0, 0))
    out = pl.pallas_call(
        functools.partial(_attn_sample_kernel, n_new=n_new),
        grid=(bd,),
        in_specs=[small, small, small] + [cspec(w) for w in GROUP_WINDOWS],
        out_specs=pl.BlockSpec((None, GROUP_HEADS, SAMPLE_ROWS, HEAD_DIM), lambda i: (i, 0, 0, 0)),
        out_shape=jax.ShapeDtypeStruct((bd, GROUP_HEADS, SAMPLE_ROWS, HEAD_DIM), F32),
        compiler_params=_params("parallel"),
        name="attn_sample",
    )(*(_per_head_rows(a, bd, n_new) for a in (q2d, k2d, v2d)), *views)
    return out[:, :, :n_new].transpose(0, 2, 1, 3).reshape(bd * n_new, GROUP_WIDTH)


def _finish_kernel(x_ref, heads_ref, conv_ref, gate_ref, wap_ref, wout_ref, n2_ref, wpq_ref, sk_ref,
                   x1_ref, h2_ref, s1t_ref, s2t_ref):
    attn = jnp.dot(heads_ref[...].astype(BF16), wap_ref[...], preferred_element_type=F32)
    mixed = gate_ref[:, :D_MODEL] * conv_ref[...] + gate_ref[:, D_MODEL:] * attn
    x1 = x_ref[...] + jnp.dot(mixed.astype(BF16), wout_ref[...], preferred_element_type=F32)
    x1_ref[...] = x1
    ms = jnp.mean(x1 * x1, axis=-1, keepdims=True)
    h2 = (x1 * lax.rsqrt(ms + RMS_EPS) * n2_ref[...]).astype(BF16)
    h2_ref[...] = h2
    qp = jnp.dot(h2, wpq_ref[...], preferred_element_type=F32).astype(BF16)
    for c in range(PEER_QCOLS // PEER_HALF):
        cs = slice(c * PEER_HALF, (c + 1) * PEER_HALF)
        head, half = divmod(c, 2)
        out_ref = s2t_ref if half else s1t_ref
        scores = lax.dot_general(sk_ref[c], qp[:, cs], (((1,), (1,)), ((), ())), preferred_element_type=F32)
        for lt in range(scores.shape[1] // LANES):
            out_ref[lt, head * N_KEYS:(head + 1) * N_KEYS, :] = scores[:, lt * LANES:(lt + 1) * LANES]


def _finish(x2d, heads, conv_out, gates, wap_bf, wout_bf, n2w, wpq_bf, subk_bf, *, tm):
    n = x2d.shape[0]
    row = lambda i: (i, 0)
    return pl.pallas_call(
        _finish_kernel,
        grid=(n // tm,),
        in_specs=[
            pl.BlockSpec((tm, D_MODEL), row),
            pl.BlockSpec((tm, GROUP_WIDTH), row),
            pl.BlockSpec((tm, D_MODEL), row),
            pl.BlockSpec((tm, 2 * D_MODEL), row),
            _full((GROUP_WIDTH, D_MODEL)),
            _full((D_MODEL, D_MODEL)),
            _full((1, D_MODEL)),
            _full((D_MODEL, PEER_QCOLS)),
            _full((2 * PEER_HEADS, N_KEYS, PEER_HALF)),
        ],
        out_specs=[
            pl.BlockSpec((tm, D_MODEL), row),
            pl.BlockSpec((tm, D_MODEL), row),
            pl.BlockSpec((tm // LANES, PEER_ROWS, LANES), lambda i: (i, 0, 0)),
            pl.BlockSpec((tm // LANES, PEER_ROWS, LANES), lambda i: (i, 0, 0)),
        ],
        out_shape=[
            jax.ShapeDtypeStruct((n, D_MODEL), F32),
            jax.ShapeDtypeStruct((n, D_MODEL), BF16),
            jax.ShapeDtypeStruct((n // LANES, PEER_ROWS, LANES), F32),
            jax.ShapeDtypeStruct((n // LANES, PEER_ROWS, LANES), F32),
        ],
        compiler_params=_params("parallel"),
        name="finish",
    )(x2d, heads, conv_out, gates, wap_bf, wout_bf, n2w, wpq_bf, subk_bf)


def _sorting_network(n):
    def merge(lo, hi, r):
        step = r * 2
        if step < hi - lo:
            yield from merge(lo, hi, step)
            yield from merge(lo + r, hi, step)
            yield from ((i, i + r) for i in range(lo + r, hi - r, step))
        else:
            yield (lo, lo + r)

    def sort(lo, hi):
        if hi > lo:
            mid = lo + (hi - lo) // 2
            yield from sort(lo, mid)
            yield from sort(mid + 1, hi)
            yield from merge(lo, hi, 1)

    return tuple(sort(0, n - 1))


_SORT16 = _sorting_network(PEER_TOPK)
VREGS_PER_KEYSET = N_KEYS // SUBLANES


def _replicated_max(x):
    return jnp.broadcast_to(jnp.max(x, axis=0, keepdims=True), x.shape)


def _replicated_min(x):
    return jnp.broadcast_to(jnp.min(x, axis=0, keepdims=True), x.shape)


def _top16(tiles):
    v = list(tiles)
    for a, b in _SORT16:
        v[a], v[b] = jnp.maximum(v[a], v[b]), jnp.minimum(v[a], v[b])
    tops = []
    for r in range(PEER_TOPK):
        m = _replicated_max(v[0])
        tops.append(m)
        hit = v[0] == m
        for k in range(PEER_TOPK - 1 - r):
            v[k] = jnp.where(hit, v[k + 1], v[k])
    return tops


def _peer_prep_kernel(s1t_ref, s2t_ref, phi_ref, c1_ref, e2_ref):
    sub = lax.broadcasted_iota(jnp.int32, (SUBLANES, LANES), 0)
    neg = jnp.full((SUBLANES, LANES), NEG_INF, F32)
    pos = jnp.full((SUBLANES, LANES), -NEG_INF, F32)

    def stack8(vals):
        out = vals[0]
        for b in range(1, SUBLANES):
            out = jnp.where(sub == b, vals[b], out)
        return out

    def head(h, carry):
        base = pl.multiple_of(h * N_KEYS, N_KEYS)
        rows = [pl.ds(base + k * SUBLANES, SUBLANES) for k in range(VREGS_PER_KEYSET)]
        s1 = [s1t_ref[r, :] for r in rows]
        s2 = [s2t_ref[r, :] for r in rows]
        top1 = _top16(s1)
        top2 = _top16(s2)
        a_lo, a_hi = stack8(top1[:SUBLANES]), stack8(top1[SUBLANES:])
        b_lo, b_hi = stack8(top2[:SUBLANES]), stack8(top2[SUBLANES:])
        cands = [top1[0] + b_lo, top1[0] + b_hi, top1[1] + b_lo, top1[2] + b_lo, top1[3] + b_lo]
        seen = sub < 4
        cands += [jnp.where(seen, neg, a_lo + top2[0]), a_hi + top2[0]]
        cands += [jnp.where(seen, neg, a_lo + top2[b]) for b in (1, 2, 3)]
        tops = _top16(cands + [neg] * (PEER_TOPK - len(cands)))
        best, tau = tops[0], tops[-1]
        z = jnp.zeros((SUBLANES, LANES), F32)
        for c in cands:
            z = z + jnp.where(c >= tau, jnp.exp(c - best), 0.0)
        inv_z = 1.0 / jnp.broadcast_to(jnp.sum(z, axis=0, keepdims=True), z.shape)
        phis = []
        for a in range(PEER_TOPK):
            t = jnp.where(top1[a] + b_lo >= tau, b_lo, pos)
            if a == 0:
                t = jnp.minimum(t, jnp.where(top1[a] + b_hi >= tau, b_hi, pos))
            phis.append(_replicated_min(t))
        for k, r in enumerate(rows):
            phi = pos
            for a in range(PEER_TOPK):
                phi = jnp.where(s1[k] == top1[a], phis[a], phi)
            phi_ref[r, :] = phi
            c1_ref[r, :] = jnp.exp(s1[k] - top1[0]) * inv_z
            e2_ref[r, :] = jnp.exp(s2[k] - top2[0])
        return carry

    lax.fori_loop(0, PEER_HEADS, head, 0)


def _peer_prep(s1t, s2t):
    slabs = s1t.shape[0]
    spec = pl.BlockSpec((None, PEER_ROWS, LANES), lambda i: (i, 0, 0))
    return pl.pallas_call(
        _peer_prep_kernel,
        grid=(slabs,),
        in_specs=[spec, spec],
        out_specs=[spec] * 3,
        out_shape=[jax.ShapeDtypeStruct((slabs, PEER_ROWS, LANES), F32)] * 3,
        compiler_params=_params("parallel"),
        name="peer_prep",
    )(s1t, s2t)


def _gelu(a):
    return 0.5 * a * (1.0 + lax.erf(a * np.float32(1.0 / np.sqrt(2.0))))


def _peer_dense_kernel(h2_ref, u_ref, vt_ref, phi_ref, c1_ref, s2_ref, e2_ref, x1_ref,
                       y_ref, acc_ref, at_ref, wg_ref, *, keys_per_step):
    j = pl.program_id(1)
    tn = h2_ref.shape[0]

    @pl.when(j == 0)
    def _():
        acc_ref[...] = jnp.zeros_like(acc_ref)

    at_ref[...] = lax.dot_general(u_ref[...], h2_ref[...], (((1,), (1,)), ((), ())),
                                  preferred_element_type=F32)

    def first_key(ii, carry):
        i = j * keys_per_step + ii
        out0 = pl.multiple_of(ii * N_KEYS, N_KEYS)
        for lt in range(tn // LANES):
            lanes = slice(lt * LANES, (lt + 1) * LANES)
            w = [None] * VREGS_PER_KEYSET
            for h in range(PEER_HEADS):
                row_i = pl.ds(h * N_KEYS + i, 1)
                phi = jnp.broadcast_to(phi_ref[lt, row_i, :], (SUBLANES, LANES))
                c1 = jnp.broadcast_to(c1_ref[lt, row_i, :], (SUBLANES, LANES))
                for kb in range(VREGS_PER_KEYSET):
                    rows = slice(h * N_KEYS + kb * SUBLANES, h * N_KEYS + (kb + 1) * SUBLANES)
                    t = jnp.where(s2_ref[lt, rows, :] >= phi, e2_ref[lt, rows, :], 0.0) * c1
                    w[kb] = t if h == 0 else w[kb] + t
            for kp in range(VREGS_PER_KEYSET // 2):
                rows = pl.ds(out0 + kp * 2 * SUBLANES, 2 * SUBLANES)
                pair = jnp.concatenate([w[2 * kp], w[2 * kp + 1]], axis=0)
                wg_ref[rows, lanes] = (pair * _gelu(at_ref[rows, lanes])).astype(BF16)
        return carry

    lax.fori_loop(0, keys_per_step, first_key, 0)
    acc_ref[...] += jnp.dot(vt_ref[...], wg_ref[...], preferred_element_type=F32)

    @pl.when(j == pl.num_programs(1) - 1)
    def _():
        y_ref[...] = x1_ref[...] + acc_ref[...].T


def _peer_dense(h2, eu_bf, evt_bf, phi, c1, s2t, e2, x1, *, tn, keys_per_step):
    n = h2.shape[0]
    ec = keys_per_step * N_KEYS
    tok = lambda i, j: (i, 0)
    slab = pl.BlockSpec((tn // LANES, PEER_ROWS, LANES), lambda i, j: (i, 0, 0))
    return pl.pallas_call(
        functools.partial(_peer_dense_kernel, keys_per_step=keys_per_step),
        grid=(n // tn, N_EXPERTS // ec),
        in_specs=[
            pl.BlockSpec((tn, D_MODEL), tok),
            pl.BlockSpec((ec, D_MODEL), lambda i, j: (j, 0)),
            pl.BlockSpec((D_MODEL, ec), lambda i, j: (0, j)),
            slab, slab, slab, slab,
            pl.BlockSpec((tn, D_MODEL), tok),
        ],
        out_specs=pl.BlockSpec((tn, D_MODEL), tok),
        out_shape=jax.ShapeDtypeStruct((n, D_MODEL), F32),
        scratch_shapes=[pltpu.VMEM((D_MODEL, tn), F32),
                        pltpu.VMEM((ec, tn), F32),
                        pltpu.VMEM((ec, tn), BF16)],
        compiler_params=_params("parallel", "arbitrary"),
        name="peer_dense",
    )(h2, eu_bf, evt_bf, phi, c1, s2t, e2, x1)


def _token_tail(x2d, heads, conv_out, gates, lw, *, tm, tn, keys_per_step):
    x1, h2, s1t, s2t = _finish(x2d, heads, conv_out, gates, lw["wap"], lw["wout"], lw["n2"], lw["wpq"],
                               lw["subk"], tm=tm)
    phi, c1, e2 = _peer_prep(s1t, s2t)
    return _peer_dense(h2, lw["eu"], lw["evt"], phi, c1, s2t, e2, x1, tn=tn, keys_per_step=keys_per_step)


def kernel(x_prompt, x_sample, cache_kv_g1, cache_kv_g2, cache_kv_g3, state_conv,
           norm1_w, w_in, q_norm_w, k_norm_w, conv_w, conv_b, conv_ln_g, conv_ln_b,
           w_conv_proj, w_attn_proj, w_out, norm2_w, w_peer_q, peer_sub_keys,
           expert_u, expert_v):
    depth = w_in.shape[0]
    bsz, seq, _ = x_prompt.shape
    bd, n_new, _ = x_sample.shape
    past_len = cache_kv_g3.shape[2]
    caches = (cache_kv_g1, cache_kv_g2, cache_kv_g3)

    seg = _head_mean_matrix()
    cos_p, sin_p = _rope_tables(np.arange(seq))
    cos_s, sin_s = _rope_tables(past_len + np.arange(bd * n_new) % n_new)
    q_scale = HEAD_DIM ** -0.5

    hp = x_prompt.reshape(bsz * seq, D_MODEL)
    hs = x_sample.reshape(bd * n_new, D_MODEL)
    kv_p = [[], [], []]
    kv_s = [[], [], []]
    conv_p, conv_s = [], []
    for layer in range(depth):
        tile = lambda w: jnp.tile(w[layer], GROUP_HEADS)[None, :]
        lw = dict(
            wap=w_attn_proj[layer].astype(BF16), wout=w_out[layer].astype(BF16),
            n2=norm2_w[layer][None, :], wpq=w_peer_q[layer].astype(BF16),
            subk=peer_sub_keys[layer].reshape(2 * PEER_HEADS, N_KEYS, PEER_HALF).astype(BF16),
            eu=expert_u[layer].astype(BF16), evt=expert_v[layer].T.astype(BF16))
        w_in_bf = w_in[layer].astype(BF16)
        n1 = norm1_w[layer][None, :]
        qnw, knw = tile(q_norm_w), tile(k_norm_w)
        conv_args = (conv_w[layer], conv_b[layer][None, :], conv_ln_g[layer][None, :],
                     conv_ln_b[layer][None, :], w_conv_proj[layer].astype(BF16))

        q, k, v, u, gates = _mixer_in(hp, seq, cos_p, sin_p, n1, w_in_bf, qnw, knw, seg,
                                      tm=512, q_dtype=BF16, q_scale=q_scale)
        u3 = u.reshape(bsz, seq, CONV_DIM)
        c_out = _conv_prompt(u3, *conv_args, tm=512).reshape(bsz * seq, D_MODEL)
        q3, k3, v3 = (a.reshape(bsz, seq, ATTN_WIDTH) for a in (q, k, v))
        outs, lses = [], []
        for g in range(N_GROUPS):
            o, l = _attn_prompt_group(q3, k3, v3, g)
            outs.append(o.reshape(bsz * seq, GROUP_WIDTH))
            lses.append(l.reshape(bsz * seq, GROUP_WIDTH))
        heads = _combine_groups(outs, lses, tm=1024)
        hp = _token_tail(hp, heads, c_out, gates, lw, tm=512, tn=512, keys_per_step=8)
        conv_p.append(u3[:, seq - (CONV_WIDTH - 1):])
        for g in range(N_GROUPS):
            keep = min(GROUP_WINDOWS[g], seq)
            gs = slice(g * GROUP_WIDTH, (g + 1) * GROUP_WIDTH)
            kv = jnp.stack([k3[:, seq - keep:, gs], v3[:, seq - keep:, gs]], axis=2)
            kv_p[g].append(kv.reshape(bsz, keep, 2, GROUP_HEADS, HEAD_DIM))

        q, k, v, u, gates = _mixer_in(hs, bd * n_new, cos_s, sin_s, n1, w_in_bf, qnw, knw, seg,
                                      tm=bd * n_new, q_dtype=F32, q_scale=q_scale)
        u3 = u.reshape(bd, n_new, CONV_DIM)
        xp = jnp.concatenate([state_conv[layer], u3], axis=1)
        c_out = _conv_sample(xp.transpose(1, 0, 2), *conv_args)
        c_out = c_out.transpose(1, 0, 2).reshape(bd * n_new, D_MODEL)
        heads = _attn_sample(q, k, v, *(c[layer] for c in caches), n_new=n_new)
        hs = _token_tail(hs, heads, c_out, gates, lw, tm=bd * n_new, tn=bd * n_new, keys_per_step=8)
        conv_s.append(xp[:, n_new:])
        k3, v3 = (a.reshape(bd, n_new, ATTN_WIDTH) for a in (k, v))
        for g in range(N_GROUPS):
            gs = slice(g * GROUP_WIDTH, (g + 1) * GROUP_WIDTH)
            kv = jnp.stack([k3[:, :, gs], v3[:, :, gs]], axis=2)
            kv_s[g].append(kv.reshape(bd, n_new, 2, GROUP_HEADS, HEAD_DIM))

    return (hp.reshape(bsz, seq, D_MODEL), hs.reshape(bd, n_new, D_MODEL),
            jnp.stack(kv_p[0]), jnp.stack(kv_p[1]), jnp.stack(kv_p[2]), jnp.stack(conv_p),
            jnp.stack(kv_s[0]), jnp.stack(kv_s[1]), jnp.stack(kv_s[2]), jnp.stack(conv_s))
```

```python
import functools

import numpy as np
import jax
import jax.numpy as jnp
from jax import lax
from jax.experimental import pallas as pl
from jax.experimental.pallas import tpu as pltpu

F32 = jnp.float32
BF16 = jnp.bfloat16

D_MODEL = 1024
HEAD_DIM = 64
HALF_DIM = HEAD_DIM // 2
GROUP_HEADS = 8
N_GROUPS = 3
GROUP_WIDTH = GROUP_HEADS * HEAD_DIM
ATTN_WIDTH = N_GROUPS * GROUP_WIDTH
GROUP_WINDOWS = (128, 512, 2048)
GROUP_DILATIONS = (1, 4, 16)
WINDOW_STEPS = 128
ROPE_THETA = 10000.0
CONV_DIM = D_MODEL
CONV_WIDTH = 31
CONV_HALO = 32
PEER_HEADS = 8
N_KEYS = 128
N_EXPERTS = N_KEYS * N_KEYS
PEER_TOPK = 16
PEER_HALF = 128
PEER_QCOLS = PEER_HEADS * 2 * PEER_HALF
PEER_ROWS = PEER_HEADS * N_KEYS
LANES = 128
SUBLANES = 8
RMS_EPS = 1e-6
LN_EPS = 1e-5
NEG_INF = float("-inf")

VMEM_LIMIT = 56 * 1024 * 1024


def _params(*sem):
    return pltpu.CompilerParams(dimension_semantics=sem, vmem_limit_bytes=VMEM_LIMIT)


def _full(shape):
    return pl.BlockSpec(shape, lambda *_: (0,) * len(shape), pipeline_mode=pl.Buffered(1))


def _split_bf16(x):
    hi = x.astype(BF16)
    lo = (x - hi.astype(F32)).astype(BF16)
    return hi, lo


def _rope_tables(positions):
    inv_freq = ROPE_THETA ** (-np.arange(HALF_DIM, dtype=np.float64) / HALF_DIM)
    ang = np.asarray(positions, np.float64)[:, None] * inv_freq[None, :]
    cos = np.cos(ang)
    sin = np.sin(ang)
    cos_t = np.concatenate([cos, cos, cos, cos], axis=1)
    sin_t = np.concatenate([-sin, sin, -sin, sin], axis=1)
    return jnp.asarray(cos_t, F32), jnp.asarray(sin_t, F32)


def _head_mean_matrix():
    idx = np.arange(GROUP_WIDTH) // HEAD_DIM
    return jnp.asarray((idx[:, None] == idx[None, :]) / HEAD_DIM, BF16)


def _mixer_in_kernel(x_ref, n1_ref, w_ref, cos_ref, sin_ref, qn_ref, kn_ref, seg_ref,
                     q_ref, k_ref, v_ref, u_ref, g_ref, *rest, q_scale):
    tm = x_ref.shape[0]
    x = x_ref[...]
    ms = jnp.mean(x * x, axis=-1, keepdims=True)
    h = (x * lax.rsqrt(ms + RMS_EPS) * n1_ref[...]).astype(BF16)
    cos = jnp.concatenate([cos_ref[...]] * 4, axis=1)
    sin = jnp.concatenate([sin_ref[...]] * 4, axis=1)
    lane = lax.broadcasted_iota(jnp.int32, (tm, GROUP_WIDTH), 1)
    first_half = (lane % HEAD_DIM) < HALF_DIM

    def proj(c0, width):
        return jnp.dot(h, w_ref[:, c0:c0 + width], preferred_element_type=F32)

    def normed_rotated(c0, nw_ref):
        z = proj(c0, GROUP_WIDTH)
        msq = jnp.dot((z * z).astype(BF16), seg_ref[...], preferred_element_type=F32)
        zn = z * lax.rsqrt(msq + RMS_EPS) * nw_ref[...]
        swapped = jnp.where(first_half,
                            pltpu.roll(zn, GROUP_WIDTH - HALF_DIM, 1),
                            pltpu.roll(zn, HALF_DIM, 1))
        return zn * cos + swapped * sin

    def by_residue(val, out_ref, scr_ref):
        d = out_ref.shape[0]
        for c in range(GROUP_WIDTH // LANES):
            scr_ref[c] = val[:, c * LANES:(c + 1) * LANES]
        for r in range(d):
            for c in range(GROUP_WIDTH // LANES):
                out_ref[r, :, c * LANES:(c + 1) * LANES] = (
                    scr_ref[c, pl.ds(r, tm // d, stride=d), :].astype(out_ref.dtype))

    for g in range(N_GROUPS):
        cs = slice(g * GROUP_WIDTH, (g + 1) * GROUP_WIDTH)
        qg = normed_rotated(g * GROUP_WIDTH, qn_ref) * q_scale
        kg = normed_rotated(ATTN_WIDTH + g * GROUP_WIDTH, kn_ref)
        vg = proj(2 * ATTN_WIDTH + g * GROUP_WIDTH, GROUP_WIDTH)
        q_ref[:, cs] = qg.astype(q_ref.dtype)
        k_ref[:, cs] = kg
        v_ref[:, cs] = vg
        if rest and g > 0:
            for val, out_ref in zip((qg, kg, vg), rest[3 * (g - 1):3 * g]):
                by_residue(val, out_ref, rest[-1])
    glu0 = 3 * ATTN_WIDTH
    for c in range(CONV_DIM // GROUP_WIDTH):
        cs = slice(c * GROUP_WIDTH, (c + 1) * GROUP_WIDTH)
        a = proj(glu0 + c * GROUP_WIDTH, GROUP_WIDTH)
        b = proj(glu0 + CONV_DIM + c * GROUP_WIDTH, GROUP_WIDTH)
        u_ref[:, cs] = a * jax.nn.sigmoid(b)
    gate0 = glu0 + 2 * CONV_DIM
    for c in range(2 * D_MODEL // GROUP_WIDTH):
        cs = slice(c * GROUP_WIDTH, (c + 1) * GROUP_WIDTH)
        g_ref[:, cs] = jax.nn.sigmoid(proj(gate0 + c * GROUP_WIDTH, GROUP_WIDTH))


def _mixer_in(x2d, positions_period, cos_t, sin_t, n1w, w_in_bf, qnw, knw, seg, *, tm, q_dtype, q_scale,
              by_residue=False):
    n = x2d.shape[0]
    in_cols = w_in_bf.shape[1]
    period_tiles = positions_period // tm
    row = lambda i: (i, 0)
    extra_specs, extra_shapes, scratch = [], [], []
    if by_residue:
        for d in GROUP_DILATIONS[1:]:
            spec = pl.BlockSpec((None, d, tm // d, GROUP_WIDTH),
                                lambda i: (i // period_tiles, 0, i % period_tiles, 0))
            shape = jax.ShapeDtypeStruct((n // positions_period, d, positions_period // d, GROUP_WIDTH), BF16)
            extra_specs += [spec] * 3
            extra_shapes += [shape] * 3
        scratch = [pltpu.VMEM((GROUP_WIDTH // LANES, tm, LANES), F32)]
    return pl.pallas_call(
        functools.partial(_mixer_in_kernel, q_scale=q_scale),
        grid=(n // tm,),
        in_specs=[
            pl.BlockSpec((tm, D_MODEL), row),
            _full((1, D_MODEL)),
            _full((D_MODEL, in_cols)),
            pl.BlockSpec((tm, 128), lambda i: (i % period_tiles, 0)),
            pl.BlockSpec((tm, 128), lambda i: (i % period_tiles, 0)),
            _full((1, GROUP_WIDTH)),
            _full((1, GROUP_WIDTH)),
            _full((GROUP_WIDTH, GROUP_WIDTH)),
        ],
        out_specs=[
            pl.BlockSpec((tm, ATTN_WIDTH), row),
            pl.BlockSpec((tm, ATTN_WIDTH), row),
            pl.BlockSpec((tm, ATTN_WIDTH), row),
            pl.BlockSpec((tm, CONV_DIM), row),
            pl.BlockSpec((tm, 2 * D_MODEL), row),
        ] + extra_specs,
        out_shape=[
            jax.ShapeDtypeStruct((n, ATTN_WIDTH), q_dtype),
            jax.ShapeDtypeStruct((n, ATTN_WIDTH), F32),
            jax.ShapeDtypeStruct((n, ATTN_WIDTH), F32),
            jax.ShapeDtypeStruct((n, CONV_DIM), F32),
            jax.ShapeDtypeStruct((n, 2 * D_MODEL), F32),
        ] + extra_shapes,
        scratch_shapes=scratch,
        compiler_params=_params("parallel"),
        name="mixer_in",
    )(x2d, n1w, w_in_bf, cos_t, sin_t, qnw, knw, seg)


CONV_ROWS = 32
CONV_LANES = 256


def _conv_tail(y, b_ref, lg_ref, lb_ref, wp_ref):
    y = y + b_ref[...]
    mu = jnp.mean(y, axis=-1, keepdims=True)
    yc = y - mu
    var = jnp.mean(yc * yc, axis=-1, keepdims=True)
    yn = yc * lax.rsqrt(var + LN_EPS) * lg_ref[...] + lb_ref[...]
    act = yn * jax.nn.sigmoid(yn)
    return jnp.dot(act.astype(BF16), wp_ref[...], preferred_element_type=F32)


def _conv_prompt_kernel(u_ref, prev_ref, cw_ref, b_ref, lg_ref, lb_ref, wp_ref, o_ref, xs_ref, y_ref):
    tm = u_ref.shape[0]
    i = pl.program_id(1)
    prev = prev_ref[...]
    xs_ref[0:CONV_HALO, :] = jnp.where(i == 0, jnp.zeros_like(prev), prev)
    xs_ref[CONV_HALO:, :] = u_ref[...]
    first = CONV_HALO - (CONV_WIDTH - 1)

    def rows(r, carry):
        base = pl.multiple_of(r * CONV_ROWS, CONV_ROWS)
        for c in range(CONV_DIM // CONV_LANES):
            cs = slice(c * CONV_LANES, (c + 1) * CONV_LANES)
            window = xs_ref[pl.ds(base, CONV_ROWS + CONV_HALO), cs]
            acc = jnp.zeros((CONV_ROWS, CONV_LANES), F32)
            for r in range(SUBLANES):
                taps = [w for w in range(CONV_WIDTH) if (first + w) % SUBLANES == r]
                shifted = window if r == 0 else pltpu.roll(window, CONV_ROWS + CONV_HALO - r, 0)
                for w in taps:
                    q = (first + w) // SUBLANES * SUBLANES
                    acc = acc + shifted[q:q + CONV_ROWS, :] * cw_ref[w:w + 1, cs]
            y_ref[pl.ds(base, CONV_ROWS), cs] = acc
        return carry

    lax.fori_loop(0, tm // CONV_ROWS, rows, 0)
    o_ref[...] = _conv_tail(y_ref[...], b_ref, lg_ref, lb_ref, wp_ref)


def _conv_prompt(u3d, cw, cb, lg, lb, wp_bf, *, tm):
    bsz, t, _ = u3d.shape
    halo_per_tile = tm // CONV_HALO
    return pl.pallas_call(
        _conv_prompt_kernel,
        grid=(bsz, t // tm),
        in_specs=[
            pl.BlockSpec((None, tm, CONV_DIM), lambda b, i: (b, i, 0)),
            pl.BlockSpec((None, CONV_HALO, CONV_DIM),
                         lambda b, i: (b, jnp.maximum(i * halo_per_tile - 1, 0), 0)),
            _full((CONV_WIDTH, CONV_DIM)),
            _full((1, CONV_DIM)), _full((1, CONV_DIM)), _full((1, CONV_DIM)),
            _full((CONV_DIM, D_MODEL)),
        ],
        out_specs=pl.BlockSpec((None, tm, D_MODEL), lambda b, i: (b, i, 0)),
        out_shape=jax.ShapeDtypeStruct((bsz, t, D_MODEL), F32),
        scratch_shapes=[pltpu.VMEM((tm + CONV_HALO, CONV_DIM), F32),
                        pltpu.VMEM((tm, CONV_DIM), F32)],
        compiler_params=_params("parallel", "arbitrary"),
        name="conv_prompt",
    )(u3d, u3d, cw, cb, lg, lb, wp_bf)


def _conv_sample_kernel(xp_ref, cw_ref, b_ref, lg_ref, lb_ref, wp_ref, o_ref):
    n_new = o_ref.shape[0]
    for s in range(n_new):
        acc = jnp.zeros(xp_ref.shape[1:], F32)
        for w in range(CONV_WIDTH):
            acc = acc + xp_ref[s + w] * cw_ref[w:w + 1, :]
        o_ref[s] = _conv_tail(acc, b_ref, lg_ref, lb_ref, wp_ref)


def _conv_sample(xp_t, cw, cb, lg, lb, wp_bf):
    rows, bd, _ = xp_t.shape
    n_new = rows - (CONV_WIDTH - 1)
    return pl.pallas_call(
        _conv_sample_kernel,
        grid=(1,),
        in_specs=[_full(xp_t.shape), _full((CONV_WIDTH, CONV_DIM)),
                  _full((1, CONV_DIM)), _full((1, CONV_DIM)), _full((1, CONV_DIM)),
                  _full((CONV_DIM, D_MODEL))],
        out_specs=pl.BlockSpec((n_new, bd, D_MODEL), lambda i: (0, 0, 0)),
        out_shape=jax.ShapeDtypeStruct((n_new, bd, D_MODEL), F32),
        compiler_params=_params("arbitrary"),
        name="conv_sample",
    )(xp_t, cw, cb, lg, lb, wp_bf)


def _attn_prompt_kernel(q_ref, kp_ref, kc_ref, vp_ref, vc_ref, o_ref, l_ref):
    blk = q_ref.shape[0]
    n = pl.program_id(2)
    qi = lax.broadcasted_iota(jnp.int32, (blk, 2 * blk), 0)
    kj = lax.broadcasted_iota(jnp.int32, (blk, 2 * blk), 1)
    first_key = jnp.where(n > 0, qi, blk)
    valid = (kj >= first_key) & (kj <= qi + blk)
    outs, lses = [], []
    for h in range(GROUP_HEADS):
        cs = slice(h * HEAD_DIM, (h + 1) * HEAD_DIM)
        q = q_ref[:, cs]
        k = jnp.concatenate([kp_ref[:, cs], kc_ref[:, cs]], axis=0).astype(BF16)
        v = jnp.concatenate([vp_ref[:, cs], vc_ref[:, cs]], axis=0).astype(BF16)
        s = lax.dot_general(q, k, (((1,), (1,)), ((), ())), preferred_element_type=F32)
        s = jnp.where(valid, s, NEG_INF)
        m = jnp.max(s, axis=-1, keepdims=True)
        p = jnp.exp(s - m)
        l = jnp.sum(p, axis=-1, keepdims=True)
        o = jnp.dot(p.astype(BF16), v, preferred_element_type=F32) / l
        outs.append(o)
        lses.append(jnp.broadcast_to(m + jnp.log(l), (blk, HEAD_DIM)))
    o_ref[...] = jnp.concatenate(outs, axis=1)
    l_ref[...] = jnp.concatenate(lses, axis=1)


def _attn_prompt_group(q3d, k3d, v3d, g):
    bsz, t, _ = q3d.shape
    d = GROUP_DILATIONS[g]
    m = t // d
    blk = WINDOW_STEPS
    assert m % blk == 0
    nb = m // blk
    view = lambda a: a.reshape(bsz, m, d * ATTN_WIDTH)
    cur = lambda b, r, n: (b, n, r * N_GROUPS + g)
    prv = lambda b, r, n: (b, jnp.maximum(n - 1, 0), r * N_GROUPS + g)
    blk_spec = lambda im: pl.BlockSpec((None, blk, GROUP_WIDTH), im)
    o, lse = pl.pallas_call(
        _attn_prompt_kernel,
        grid=(bsz, d, nb),
        in_specs=[blk_spec(cur), blk_spec(prv), blk_spec(cur), blk_spec(prv), blk_spec(cur)],
        out_specs=[pl.BlockSpec((None, blk, GROUP_WIDTH), lambda b, r, n: (b, n, r))] * 2,
        out_shape=[jax.ShapeDtypeStruct((bsz, m, d * GROUP_WIDTH), F32)] * 2,
        compiler_params=_params("parallel", "parallel", "arbitrary"),
        name=f"attn_prompt_g{g + 1}",
    )(view(q3d), view(k3d), view(k3d), view(v3d), view(v3d))
    return o.reshape(bsz, t, GROUP_WIDTH), lse.reshape(bsz, t, GROUP_WIDTH)


def _attn_prompt_residue(qr, kr, vr, g):
    bsz, d, m, _ = qr.shape
    blk = WINDOW_STEPS
    assert d == GROUP_DILATIONS[g] and m % blk == 0
    cur = lambda b, r, n: (b, r, n, 0)
    prv = lambda b, r, n: (b, r, jnp.maximum(n - 1, 0), 0)
    blk_spec = lambda im: pl.BlockSpec((None, None, blk, GROUP_WIDTH), im)
    return pl.pallas_call(
        _attn_prompt_kernel,
        grid=(bsz, d, m // blk),
        in_specs=[blk_spec(cur), blk_spec(prv), blk_spec(cur), blk_spec(prv), blk_spec(cur)],
        out_specs=[blk_spec(cur)] * 2,
        out_shape=[jax.ShapeDtypeStruct((bsz, d, m, GROUP_WIDTH), F32)] * 2,
        compiler_params=_params("parallel", "parallel", "arbitrary"),
        name=f"attn_prompt_g{g + 1}",
    )(qr, kr, kr, vr, vr)


def _combine_kernel(o1, l1, o2, l2, o3, l3, out_ref, scr_ref):
    tm = out_ref.shape[0]

    def natural(x_ref):
        d = x_ref.shape[0]
        for r in range(d):
            for c in range(GROUP_WIDTH // LANES):
                scr_ref[c, pl.ds(r, tm // d, stride=d), :] = x_ref[r, :, c * LANES:(c + 1) * LANES]
        return jnp.concatenate([scr_ref[c] for c in range(GROUP_WIDTH // LANES)], axis=1)

    la, oa = l1[...], o1[...]
    lb, ob = natural(l2), natural(o2)
    lc, oc = natural(l3), natural(o3)
    mx = jnp.maximum(jnp.maximum(la, lb), lc)
    ea, eb, ec = jnp.exp(la - mx), jnp.exp(lb - mx), jnp.exp(lc - mx)
    out_ref[...] = (ea * oa + eb * ob + ec * oc) / (ea + eb + ec)


def _combine_groups(o1, l1, o2, l2, o3, l3, *, tm):
    bsz, t, _ = o1.shape
    nat = pl.BlockSpec((None, tm, GROUP_WIDTH), lambda b, i: (b, i, 0))
    res = lambda d: pl.BlockSpec((None, d, tm // d, GROUP_WIDTH), lambda b, i: (b, 0, i, 0))
    d2, d3 = GROUP_DILATIONS[1:]
    out = pl.pallas_call(
        _combine_kernel,
        grid=(bsz, t // tm),
        in_specs=[nat, nat, res(d2), res(d2), res(d3), res(d3)],
        out_specs=nat,
        out_shape=jax.ShapeDtypeStruct((bsz, t, GROUP_WIDTH), F32),
        scratch_shapes=[pltpu.VMEM((GROUP_WIDTH // LANES, tm, LANES), F32)],
        compiler_params=_params("parallel", "parallel"),
        name="combine_groups",
    )(o1, l1, o2, l2, o3, l3)
    return out.reshape(bsz * t, GROUP_WIDTH)


SAMPLE_ROWS = 8


def _attn_sample_kernel(q_ref, kn_ref, vn_ref, c1_ref, c2_ref, c3_ref, o_ref, *, n_new):
    row = lax.broadcasted_iota(jnp.int32, (GROUP_HEADS, SAMPLE_ROWS, 1), 1)
    stats = []
    for g, c_ref in enumerate((c1_ref, c2_ref, c3_ref)):
        d = GROUP_DILATIONS[g]
        span = c_ref.shape[-1]
        q = q_ref[g]
        k_old = c_ref[0].astype(BF16)
        v_old = c_ref[1].astype(BF16)
        s_old = lax.dot_general(q.astype(BF16), k_old, (((2,), (1,)), ((0,), (0,))),
                                preferred_element_type=F32)
        pos = lax.broadcasted_iota(jnp.int32, (GROUP_HEADS, SAMPLE_ROWS, span), 2)
        ok = (((pos - row) & (d - 1)) == 0) & (pos >= row)
        s_old = jnp.where(ok, s_old, NEG_INF)
        mx = jnp.max(s_old, axis=-1, keepdims=True)
        s_new = []
        for k in range(n_new):
            sk = jnp.sum(q * kn_ref[g, :, k:k + 1, :], axis=-1, keepdims=True)
            ok_new = (row >= k) if g == 0 else (row == k)
            sk = jnp.where(ok_new, sk, NEG_INF)
            s_new.append(sk)
            mx = jnp.maximum(mx, sk)
        p_old = jnp.exp(s_old - mx)
        den = jnp.sum(p_old, axis=-1, keepdims=True)
        acc = lax.dot_general(p_old.astype(BF16), v_old, (((2,), (2,)), ((0,), (0,))),
                              preferred_element_type=F32)
        for k in range(n_new):
            pk = jnp.exp(s_new[k] - mx)
            den = den + pk
            acc = acc + pk * vn_ref[g, :, k:k + 1, :]
        stats.append((mx, den, acc))
    top = jnp.maximum(jnp.maximum(stats[0][0], stats[1][0]), stats[2][0])
    total = jnp.zeros_like(top)
    out = jnp.zeros(o_ref.shape, F32)
    for mx, den, acc in stats:
        w = jnp.exp(mx - top)
        total = total + w * den
        out = out + w * acc
    o_ref[...] = out / total


def _per_head_rows(a2d, bd, n_new):
    a = a2d.reshape(bd, n_new, N_GROUPS, GROUP_HEADS, HEAD_DIM).transpose(0, 2, 3, 1, 4)
    return jnp.pad(a, ((0, 0), (0, 0), (0, 0), (0, SAMPLE_ROWS - n_new), (0, 0)))


def _attn_sample(q2d, k2d, v2d, cache1, cache2, cache3, *, n_new):
    bd = q2d.shape[0] // n_new
    assert n_new <= min(SAMPLE_ROWS, GROUP_DILATIONS[1])
    views = []
    for g, c in enumerate((cache1, cache2, cache3)):
        assert c.shape[1] == GROUP_WINDOWS[g]
        views.append(c.transpose(0, 2, 3, 4, 1))
    small = pl.BlockSpec((None, N_GROUPS, GROUP_HEADS, SAMPLE_ROWS, HEAD_DIM), lambda i: (i, 0, 0, 0, 0))
    cspec = lambda span: pl.BlockSpec((None, 2, GROUP_HEADS, HEAD_DIM, span), lambda i: (i, 0, 0, 0, 0))
    out = pl.pallas_call(
        functools.partial(_attn_sample_kernel, n_new=n_new),
        grid=(bd,),
        in_specs=[small, small, small] + [cspec(w) for w in GROUP_WINDOWS],
        out_specs=pl.BlockSpec((None, GROUP_HEADS, SAMPLE_ROWS, HEAD_DIM), lambda i: (i, 0, 0, 0)),
        out_shape=jax.ShapeDtypeStruct((bd, GROUP_HEADS, SAMPLE_ROWS, HEAD_DIM), F32),
        compiler_params=_params("parallel"),
        name="attn_sample",
    )(*(_per_head_rows(a, bd, n_new) for a in (q2d, k2d, v2d)), *views)
    return out[:, :, :n_new].transpose(0, 2, 1, 3).reshape(bd * n_new, GROUP_WIDTH)


def _finish_kernel(x_ref, heads_ref, conv_ref, gate_ref, wap_ref, wout_ref, n2_ref, wpq_ref, sk_ref,
                   x1_ref, h2_ref, s1t_ref, s2t_ref):
    attn = jnp.dot(heads_ref[...].astype(BF16), wap_ref[...], preferred_element_type=F32)
    mixed = gate_ref[:, :D_MODEL] * conv_ref[...] + gate_ref[:, D_MODEL:] * attn
    x1 = x_ref[...] + jnp.dot(mixed.astype(BF16), wout_ref[...], preferred_element_type=F32)
    x1_ref[...] = x1
    ms = jnp.mean(x1 * x1, axis=-1, keepdims=True)
    h2 = (x1 * lax.rsqrt(ms + RMS_EPS) * n2_ref[...]).astype(BF16)
    h2_ref[...] = h2
    qp = jnp.dot(h2, wpq_ref[...], preferred_element_type=F32).astype(BF16)
    for c in range(PEER_QCOLS // PEER_HALF):
        cs = slice(c * PEER_HALF, (c + 1) * PEER_HALF)
        head, half = divmod(c, 2)
        out_ref = s2t_ref if half else s1t_ref
        scores = lax.dot_general(sk_ref[c], qp[:, cs], (((1,), (1,)), ((), ())), preferred_element_type=F32)
        for lt in range(scores.shape[1] // LANES):
            out_ref[lt, head * N_KEYS:(head + 1) * N_KEYS, :] = scores[:, lt * LANES:(lt + 1) * LANES]


def _finish(x2d, heads, conv_out, gates, wap_bf, wout_bf, n2w, wpq_bf, subk_bf, *, tm):
    n = x2d.shape[0]
    row = lambda i: (i, 0)
    return pl.pallas_call(
        _finish_kernel,
        grid=(n // tm,),
        in_specs=[
            pl.BlockSpec((tm, D_MODEL), row),
            pl.BlockSpec((tm, GROUP_WIDTH), row),
            pl.BlockSpec((tm, D_MODEL), row),
            pl.BlockSpec((tm, 2 * D_MODEL), row),
            _full((GROUP_WIDTH, D_MODEL)),
            _full((D_MODEL, D_MODEL)),
            _full((1, D_MODEL)),
            _full((D_MODEL, PEER_QCOLS)),
            _full((2 * PEER_HEADS, N_KEYS, PEER_HALF)),
        ],
        out_specs=[
            pl.BlockSpec((tm, D_MODEL), row),
            pl.BlockSpec((tm, D_MODEL), row),
            pl.BlockSpec((tm // LANES, PEER_ROWS, LANES), lambda i: (i, 0, 0)),
            pl.BlockSpec((tm // LANES, PEER_ROWS, LANES), lambda i: (i, 0, 0)),
        ],
        out_shape=[
            jax.ShapeDtypeStruct((n, D_MODEL), F32),
            jax.ShapeDtypeStruct((n, D_MODEL), BF16),
            jax.ShapeDtypeStruct((n // LANES, PEER_ROWS, LANES), F32),
            jax.ShapeDtypeStruct((n // LANES, PEER_ROWS, LANES), F32),
        ],
        compiler_params=_params("parallel"),
        name="finish",
    )(x2d, heads, conv_out, gates, wap_bf, wout_bf, n2w, wpq_bf, subk_bf)


def _sorting_network(n):
    def merge(lo, hi, r):
        step = r * 2
        if step < hi - lo:
            yield from merge(lo, hi, step)
            yield from merge(lo + r, hi, step)
            yield from ((i, i + r) for i in range(lo + r, hi - r, step))
        else:
            yield (lo, lo + r)

    def sort(lo, hi):
        if hi > lo:
            mid = lo + (hi - lo) // 2
            yield from sort(lo, mid)
            yield from sort(mid + 1, hi)
            yield from merge(lo, hi, 1)

    return tuple(sort(0, n - 1))


_SORT16 = _sorting_network(PEER_TOPK)
VREGS_PER_KEYSET = N_KEYS // SUBLANES


def _replicated_max(x):
    return jnp.broadcast_to(jnp.max(x, axis=0, keepdims=True), x.shape)


def _replicated_min(x):
    return jnp.broadcast_to(jnp.min(x, axis=0, keepdims=True), x.shape)


def _top16(tiles):
    v = list(tiles)
    for a, b in _SORT16:
        v[a], v[b] = jnp.maximum(v[a], v[b]), jnp.minimum(v[a], v[b])
    tops = []
    for r in range(PEER_TOPK):
        m = _replicated_max(v[0])
        tops.append(m)
        hit = v[0] == m
        for k in range(PEER_TOPK - 1 - r):
            v[k] = jnp.where(hit, v[k + 1], v[k])
    return tops


def _peer_prep_kernel(s1t_ref, s2t_ref, phi_ref, c1_ref, e2_ref):
    sub = lax.broadcasted_iota(jnp.int32, (SUBLANES, LANES), 0)
    neg = jnp.full((SUBLANES, LANES), NEG_INF, F32)
    pos = jnp.full((SUBLANES, LANES), -NEG_INF, F32)

    def stack8(vals):
        out = vals[0]
        for b in range(1, SUBLANES):
            out = jnp.where(sub == b, vals[b], out)
        return out

    def head(h, carry):
        base = pl.multiple_of(h * N_KEYS, N_KEYS)
        rows = [pl.ds(base + k * SUBLANES, SUBLANES) for k in range(VREGS_PER_KEYSET)]
        s1 = [s1t_ref[r, :] for r in rows]
        s2 = [s2t_ref[r, :] for r in rows]
        top1 = _top16(s1)
        top2 = _top16(s2)
        a_lo, a_hi = stack8(top1[:SUBLANES]), stack8(top1[SUBLANES:])
        b_lo, b_hi = stack8(top2[:SUBLANES]), stack8(top2[SUBLANES:])
        cands = [top1[0] + b_lo, top1[0] + b_hi, top1[1] + b_lo, top1[2] + b_lo, top1[3] + b_lo]
        seen = sub < 4
        cands += [jnp.where(seen, neg, a_lo + top2[0]), a_hi + top2[0]]
        cands += [jnp.where(seen, neg, a_lo + top2[b]) for b in (1, 2, 3)]
        tops = _top16(cands + [neg] * (PEER_TOPK - len(cands)))
        best, tau = tops[0], tops[-1]
        z = jnp.zeros((SUBLANES, LANES), F32)
        for c in cands:
            z = z + jnp.where(c >= tau, jnp.exp(c - best), 0.0)
        inv_z = 1.0 / jnp.broadcast_to(jnp.sum(z, axis=0, keepdims=True), z.shape)
        phis = []
        for a in range(PEER_TOPK):
            t = jnp.where(top1[a] + b_lo >= tau, b_lo, pos)
            if a == 0:
                t = jnp.minimum(t, jnp.where(top1[a] + b_hi >= tau, b_hi, pos))
            phis.append(_replicated_min(t))
        for k, r in enumerate(rows):
            phi = pos
            for a in range(PEER_TOPK):
                phi = jnp.where(s1[k] == top1[a], phis[a], phi)
            phi_ref[r, :] = phi
            c1_ref[r, :] = jnp.exp(s1[k] - top1[0]) * inv_z
            e2_ref[r, :] = jnp.exp(s2[k] - top2[0])
        return carry

    lax.fori_loop(0, PEER_HEADS, head, 0)


def _peer_prep(s1t, s2t):
    slabs = s1t.shape[0]
    spec = pl.BlockSpec((None, PEER_ROWS, LANES), lambda i: (i, 0, 0))
    return pl.pallas_call(
        _peer_prep_kernel,
        grid=(slabs,),
        in_specs=[spec, spec],
        out_specs=[spec] * 3,
        out_shape=[jax.ShapeDtypeStruct((slabs, PEER_ROWS, LANES), F32)] * 3,
        compiler_params=_params("parallel"),
        name="peer_prep",
    )(s1t, s2t)


def _gelu(a):
    return 0.5 * a * (1.0 + lax.erf(a * np.float32(1.0 / np.sqrt(2.0))))


def _peer_dense_kernel(h2_ref, u_ref, vt_ref, phi_ref, c1_ref, s2_ref, e2_ref, x1_ref,
                       y_ref, acc_ref, at_ref, wg_ref, *, keys_per_step):
    j = pl.program_id(1)
    tn = h2_ref.shape[0]

    @pl.when(j == 0)
    def _():
        acc_ref[...] = jnp.zeros_like(acc_ref)

    at_ref[...] = lax.dot_general(u_ref[...], h2_ref[...], (((1,), (1,)), ((), ())),
                                  preferred_element_type=F32)

    def first_key(ii, carry):
        i = j * keys_per_step + ii
        out0 = pl.multiple_of(ii * N_KEYS, N_KEYS)
        for lt in range(tn // LANES):
            lanes = slice(lt * LANES, (lt + 1) * LANES)
            w = [None] * VREGS_PER_KEYSET
            for h in range(PEER_HEADS):
                row_i = pl.ds(h * N_KEYS + i, 1)
                phi = jnp.broadcast_to(phi_ref[lt, row_i, :], (SUBLANES, LANES))
                c1 = jnp.broadcast_to(c1_ref[lt, row_i, :], (SUBLANES, LANES))
                for kb in range(VREGS_PER_KEYSET):
                    rows = slice(h * N_KEYS + kb * SUBLANES, h * N_KEYS + (kb + 1) * SUBLANES)
                    t = jnp.where(s2_ref[lt, rows, :] >= phi, e2_ref[lt, rows, :], 0.0) * c1
                    w[kb] = t if h == 0 else w[kb] + t
            for kp in range(VREGS_PER_KEYSET // 2):
                rows = pl.ds(out0 + kp * 2 * SUBLANES, 2 * SUBLANES)
                pair = jnp.concatenate([w[2 * kp], w[2 * kp + 1]], axis=0)
                wg_ref[rows, lanes] = (pair * _gelu(at_ref[rows, lanes])).astype(BF16)
        return carry

    lax.fori_loop(0, keys_per_step, first_key, 0)
    acc_ref[...] += jnp.dot(vt_ref[...], wg_ref[...], preferred_element_type=F32)

    @pl.when(j == pl.num_programs(1) - 1)
    def _():
        y_ref[...] = x1_ref[...] + acc_ref[...].T


def _peer_dense(h2, eu_bf, evt_bf, phi, c1, s2t, e2, x1, *, tn, keys_per_step):
    n = h2.shape[0]
    ec = keys_per_step * N_KEYS
    tok = lambda i, j: (i, 0)
    slab = pl.BlockSpec((tn // LANES, PEER_ROWS, LANES), lambda i, j: (i, 0, 0))
    return pl.pallas_call(
        functools.partial(_peer_dense_kernel, keys_per_step=keys_per_step),
        grid=(n // tn, N_EXPERTS // ec),
        in_specs=[
            pl.BlockSpec((tn, D_MODEL), tok),
            pl.BlockSpec((ec, D_MODEL), lambda i, j: (j, 0)),
            pl.BlockSpec((D_MODEL, ec), lambda i, j: (0, j)),
            slab, slab, slab, slab,
            pl.BlockSpec((tn, D_MODEL), tok),
        ],
        out_specs=pl.BlockSpec((tn, D_MODEL), tok),
        out_shape=jax.ShapeDtypeStruct((n, D_MODEL), F32),
        scratch_shapes=[pltpu.VMEM((D_MODEL, tn), F32),
                        pltpu.VMEM((ec, tn), F32),
                        pltpu.VMEM((ec, tn), BF16)],
        compiler_params=_params("parallel", "arbitrary"),
        name="peer_dense",
    )(h2, eu_bf, evt_bf, phi, c1, s2t, e2, x1)


def _token_tail(x2d, heads, conv_out, gates, lw, *, tm, tn, keys_per_step):
    x1, h2, s1t, s2t = _finish(x2d, heads, conv_out, gates, lw["wap"], lw["wout"], lw["n2"], lw["wpq"],
                               lw["subk"], tm=tm)
    phi, c1, e2 = _peer_prep(s1t, s2t)
    return _peer_dense(h2, lw["eu"], lw["evt"], phi, c1, s2t, e2, x1, tn=tn, keys_per_step=keys_per_step)


def kernel(x_prompt, x_sample, cache_kv_g1, cache_kv_g2, cache_kv_g3, state_conv,
           norm1_w, w_in, q_norm_w, k_norm_w, conv_w, conv_b, conv_ln_g, conv_ln_b,
           w_conv_proj, w_attn_proj, w_out, norm2_w, w_peer_q, peer_sub_keys,
           expert_u, expert_v):
    depth = w_in.shape[0]
    bsz, seq, _ = x_prompt.shape
    bd, n_new, _ = x_sample.shape
    past_len = cache_kv_g3.shape[2]
    caches = (cache_kv_g1, cache_kv_g2, cache_kv_g3)

    seg = _head_mean_matrix()
    cos_p, sin_p = _rope_tables(np.arange(seq))
    cos_s, sin_s = _rope_tables(past_len + np.arange(bd * n_new) % n_new)
    q_scale = HEAD_DIM ** -0.5

    hp = x_prompt.reshape(bsz * seq, D_MODEL)
    hs = x_sample.reshape(bd * n_new, D_MODEL)
    kv_p = [[], [], []]
    kv_s = [[], [], []]
    conv_p, conv_s = [], []
    for layer in range(depth):
        tile = lambda w: jnp.tile(w[layer], GROUP_HEADS)[None, :]
        lw = dict(
            wap=w_attn_proj[layer].astype(BF16), wout=w_out[layer].astype(BF16),
            n2=norm2_w[layer][None, :], wpq=w_peer_q[layer].astype(BF16),
            subk=peer_sub_keys[layer].reshape(2 * PEER_HEADS, N_KEYS, PEER_HALF).astype(BF16),
            eu=expert_u[layer].astype(BF16), evt=expert_v[layer].T.astype(BF16))
        w_in_bf = w_in[layer].astype(BF16)
        n1 = norm1_w[layer][None, :]
        qnw, knw = tile(q_norm_w), tile(k_norm_w)
        conv_args = (conv_w[layer], conv_b[layer][None, :], conv_ln_g[layer][None, :],
                     conv_ln_b[layer][None, :], w_conv_proj[layer].astype(BF16))

        q, k, v, u, gates, *by_res = _mixer_in(hp, seq, cos_p, sin_p, n1, w_in_bf, qnw, knw, seg,
                                               tm=256, q_dtype=BF16, q_scale=q_scale, by_residue=True)
        u3 = u.reshape(bsz, seq, CONV_DIM)
        c_out = _conv_prompt(u3, *conv_args, tm=512).reshape(bsz * seq, D_MODEL)
        q3, k3, v3 = (a.reshape(bsz, seq, ATTN_WIDTH) for a in (q, k, v))
        o1, l1 = _attn_prompt_group(q3, k3, v3, 0)
        o2, l2 = _attn_prompt_residue(*by_res[0:3], 1)
        o3, l3 = _attn_prompt_residue(*by_res[3:6], 2)
        heads = _combine_groups(o1, l1, o2, l2, o3, l3, tm=512)
        hp = _token_tail(hp, heads, c_out, gates, lw, tm=512, tn=512, keys_per_step=8)
        conv_p.append(u3[:, seq - (CONV_WIDTH - 1):])
        for g in range(N_GROUPS):
            keep = min(GROUP_WINDOWS[g], seq)
            gs = slice(g * GROUP_WIDTH, (g + 1) * GROUP_WIDTH)
            kv = jnp.stack([k3[:, seq - keep:, gs], v3[:, seq - keep:, gs]], axis=2)
            kv_p[g].append(kv.reshape(bsz, keep, 2, GROUP_HEADS, HEAD_DIM))

        q, k, v, u, gates = _mixer_in(hs, bd * n_new, cos_s, sin_s, n1, w_in_bf, qnw, knw, seg,
                                      tm=bd * n_new, q_dtype=F32, q_scale=q_scale)
        u3 = u.reshape(bd, n_new, CONV_DIM)
        xp = jnp.concatenate([state_conv[layer], u3], axis=1)
        c_out = _conv_sample(xp.transpose(1, 0, 2), *conv_args)
        c_out = c_out.transpose(1, 0, 2).reshape(bd * n_new, D_MODEL)
        heads = _attn_sample(q, k, v, *(c[layer] for c in caches), n_new=n_new)
        hs = _token_tail(hs, heads, c_out, gates, lw, tm=bd * n_new, tn=bd * n_new, keys_per_step=8)
        conv_s.append(xp[:, n_new:])
        k3, v3 = (a.reshape(bd, n_new, ATTN_WIDTH) for a in (k, v))
        for g in range(N_GROUPS):
            gs = slice(g * GROUP_WIDTH, (g + 1) * GROUP_WIDTH)
            kv = jnp.stack([k3[:, :, gs], v3[:, :, gs]], axis=2)
            kv_s[g].append(kv.reshape(bd, n_new, 2, GROUP_HEADS, HEAD_DIM))

    return (hp.reshape(bsz, seq, D_MODEL), hs.reshape(bd, n_new, D_MODEL),
            jnp.stack(kv_p[0]), jnp.stack(kv_p[1]), jnp.stack(kv_p[2]), jnp.stack(conv_p),
            jnp.stack(kv_s[0]), jnp.stack(kv_s[1]), jnp.stack(kv_s[2]), jnp.stack(conv_s))
```

```python
import functools

import numpy as np
import jax
import jax.numpy as jnp
from jax import lax
from jax.experimental import pallas as pl
from jax.experimental.pallas import tpu as pltpu

F32 = jnp.float32
BF16 = jnp.bfloat16

D_MODEL = 1024
HEAD_DIM = 64
HALF_DIM = HEAD_DIM // 2
GROUP_HEADS = 8
N_GROUPS = 3
GROUP_WIDTH = GROUP_HEADS * HEAD_DIM
ATTN_WIDTH = N_GROUPS * GROUP_WIDTH
GROUP_WINDOWS = (128, 512, 2048)
GROUP_DILATIONS = (1, 4, 16)
WINDOW_STEPS = 128
ROPE_THETA = 10000.0
CONV_DIM = D_MODEL
CONV_WIDTH = 31
CONV_HALO = 32
PEER_HEADS = 8
N_KEYS = 128
N_EXPERTS = N_KEYS * N_KEYS
PEER_TOPK = 16
PEER_HALF = 128
PEER_QCOLS = PEER_HEADS * 2 * PEER_HALF
PEER_ROWS = PEER_HEADS * N_KEYS
LANES = 128
SUBLANES = 8
RMS_EPS = 1e-6
LN_EPS = 1e-5
NEG_INF = float("-inf")

VMEM_LIMIT = 56 * 1024 * 1024


def _params(*sem):
    return pltpu.CompilerParams(dimension_semantics=sem, vmem_limit_bytes=VMEM_LIMIT)


def _full(shape):
    return pl.BlockSpec(shape, lambda *_: (0,) * len(shape), pipeline_mode=pl.Buffered(1))


def _split_bf16(x):
    hi = x.astype(BF16)
    lo = (x - hi.astype(F32)).astype(BF16)
    return hi, lo


def _rope_tables(positions):
    inv_freq = ROPE_THETA ** (-np.arange(HALF_DIM, dtype=np.float64) / HALF_DIM)
    ang = np.asarray(positions, np.float64)[:, None] * inv_freq[None, :]
    cos = np.cos(ang)
    sin = np.sin(ang)
    cos_t = np.concatenate([cos, cos, cos, cos], axis=1)
    sin_t = np.concatenate([-sin, sin, -sin, sin], axis=1)
    return jnp.asarray(cos_t, F32), jnp.asarray(sin_t, F32)


def _head_mean_matrix():
    idx = np.arange(GROUP_WIDTH) // HEAD_DIM
    return jnp.asarray((idx[:, None] == idx[None, :]) / HEAD_DIM, BF16)


def _mixer_in_kernel(x_ref, n1_ref, w_ref, cos_ref, sin_ref, qn_ref, kn_ref, seg_ref,
                     q_ref, k_ref, v_ref, u_ref, g_ref, *rest, q_scale):
    tm = x_ref.shape[0]
    x = x_ref[...]
    ms = jnp.mean(x * x, axis=-1, keepdims=True)
    h = (x * lax.rsqrt(ms + RMS_EPS) * n1_ref[...]).astype(BF16)
    cos = jnp.concatenate([cos_ref[...]] * 4, axis=1)
    sin = jnp.concatenate([sin_ref[...]] * 4, axis=1)
    lane = lax.broadcasted_iota(jnp.int32, (tm, GROUP_WIDTH), 1)
    first_half = (lane % HEAD_DIM) < HALF_DIM

    def proj(c0, width):
        return jnp.dot(h, w_ref[:, c0:c0 + width], preferred_element_type=F32)

    def normed_rotated(c0, nw_ref):
        z = proj(c0, GROUP_WIDTH)
        msq = jnp.dot((z * z).astype(BF16), seg_ref[...], preferred_element_type=F32)
        zn = z * lax.rsqrt(msq + RMS_EPS) * nw_ref[...]
        swapped = jnp.where(first_half,
                            pltpu.roll(zn, GROUP_WIDTH - HALF_DIM, 1),
                            pltpu.roll(zn, HALF_DIM, 1))
        return zn * cos + swapped * sin

    def by_residue(val, out_ref, scr_ref):
        d = out_ref.shape[0]
        for c in range(GROUP_WIDTH // LANES):
            scr_ref[c] = val[:, c * LANES:(c + 1) * LANES]
        for r in range(d):
            for c in range(GROUP_WIDTH // LANES):
                out_ref[r, :, c * LANES:(c + 1) * LANES] = (
                    scr_ref[c, pl.ds(r, tm // d, stride=d), :].astype(out_ref.dtype))

    for g in range(N_GROUPS):
        cs = slice(g * GROUP_WIDTH, (g + 1) * GROUP_WIDTH)
        qg = normed_rotated(g * GROUP_WIDTH, qn_ref) * q_scale
        kg = normed_rotated(ATTN_WIDTH + g * GROUP_WIDTH, kn_ref)
        vg = proj(2 * ATTN_WIDTH + g * GROUP_WIDTH, GROUP_WIDTH)
        q_ref[:, cs] = qg.astype(q_ref.dtype)
        k_ref[:, cs] = kg
        v_ref[:, cs] = vg
        if rest and g > 0:
            for val, out_ref in zip((qg, kg, vg), rest[3 * (g - 1):3 * g]):
                by_residue(val, out_ref, rest[-1])
    glu0 = 3 * ATTN_WIDTH
    for c in range(CONV_DIM // GROUP_WIDTH):
        cs = slice(c * GROUP_WIDTH, (c + 1) * GROUP_WIDTH)
        a = proj(glu0 + c * GROUP_WIDTH, GROUP_WIDTH)
        b = proj(glu0 + CONV_DIM + c * GROUP_WIDTH, GROUP_WIDTH)
        u_ref[:, cs] = a * jax.nn.sigmoid(b)
    gate0 = glu0 + 2 * CONV_DIM
    for c in range(2 * D_MODEL // GROUP_WIDTH):
        cs = slice(c * GROUP_WIDTH, (c + 1) * GROUP_WIDTH)
        g_ref[:, cs] = jax.nn.sigmoid(proj(gate0 + c * GROUP_WIDTH, GROUP_WIDTH))


def _mixer_in(x2d, positions_period, cos_t, sin_t, n1w, w_in_bf, qnw, knw, seg, *, tm, q_dtype, q_scale,
              by_residue=False):
    n = x2d.shape[0]
    in_cols = w_in_bf.shape[1]
    period_tiles = positions_period // tm
    row = lambda i: (i, 0)
    extra_specs, extra_shapes, scratch = [], [], []
    if by_residue:
        for d in GROUP_DILATIONS[1:]:
            spec = pl.BlockSpec((None, d, tm // d, GROUP_WIDTH),
                                lambda i: (i // period_tiles, 0, i % period_tiles, 0))
            shape = jax.ShapeDtypeStruct((n // positions_period, d, positions_period // d, GROUP_WIDTH), BF16)
            extra_specs += [spec] * 3
            extra_shapes += [shape] * 3
        scratch = [pltpu.VMEM((GROUP_WIDTH // LANES, tm, LANES), F32)]
    return pl.pallas_call(
        functools.partial(_mixer_in_kernel, q_scale=q_scale),
        grid=(n // tm,),
        in_specs=[
            pl.BlockSpec((tm, D_MODEL), row),
            _full((1, D_MODEL)),
            _full((D_MODEL, in_cols)),
            pl.BlockSpec((tm, 128), lambda i: (i % period_tiles, 0)),
            pl.BlockSpec((tm, 128), lambda i: (i % period_tiles, 0)),
            _full((1, GROUP_WIDTH)),
            _full((1, GROUP_WIDTH)),
            _full((GROUP_WIDTH, GROUP_WIDTH)),
        ],
        out_specs=[
            pl.BlockSpec((tm, ATTN_WIDTH), row),
            pl.BlockSpec((tm, ATTN_WIDTH), row),
            pl.BlockSpec((tm, ATTN_WIDTH), row),
            pl.BlockSpec((tm, CONV_DIM), row),
            pl.BlockSpec((tm, 2 * D_MODEL), row),
        ] + extra_specs,
        out_shape=[
            jax.ShapeDtypeStruct((n, ATTN_WIDTH), q_dtype),
            jax.ShapeDtypeStruct((n, ATTN_WIDTH), F32),
            jax.ShapeDtypeStruct((n, ATTN_WIDTH), F32),
            jax.ShapeDtypeStruct((n, CONV_DIM), F32),
            jax.ShapeDtypeStruct((n, 2 * D_MODEL), F32),
        ] + extra_shapes,
        scratch_shapes=scratch,
        compiler_params=_params("parallel"),
        name="mixer_in",
    )(x2d, n1w, w_in_bf, cos_t, sin_t, qnw, knw, seg)


CONV_ROWS = 32
CONV_LANES = 256


def _conv_tail(y, b_ref, lg_ref, lb_ref, wp_ref):
    y = y + b_ref[...]
    mu = jnp.mean(y, axis=-1, keepdims=True)
    yc = y - mu
    var = jnp.mean(yc * yc, axis=-1, keepdims=True)
    yn = yc * lax.rsqrt(var + LN_EPS) * lg_ref[...] + lb_ref[...]
    act = yn * jax.nn.sigmoid(yn)
    return jnp.dot(act.astype(BF16), wp_ref[...], preferred_element_type=F32)


def _conv_prompt_kernel(u_ref, prev_ref, cw_ref, b_ref, lg_ref, lb_ref, wp_ref, o_ref, xs_ref, y_ref):
    tm = u_ref.shape[0]
    i = pl.program_id(1)
    prev = prev_ref[...]
    xs_ref[0:CONV_HALO, :] = jnp.where(i == 0, jnp.zeros_like(prev), prev)
    xs_ref[CONV_HALO:, :] = u_ref[...]
    first = CONV_HALO - (CONV_WIDTH - 1)

    def rows(r, carry):
        base = pl.multiple_of(r * CONV_ROWS, CONV_ROWS)
        for c in range(CONV_DIM // CONV_LANES):
            cs = slice(c * CONV_LANES, (c + 1) * CONV_LANES)
            window = xs_ref[pl.ds(base, CONV_ROWS + CONV_HALO), cs]
            acc = jnp.zeros((CONV_ROWS, CONV_LANES), F32)
            for r in range(SUBLANES):
                taps = [w for w in range(CONV_WIDTH) if (first + w) % SUBLANES == r]
                shifted = window if r == 0 else pltpu.roll(window, CONV_ROWS + CONV_HALO - r, 0)
                for w in taps:
                    q = (first + w) // SUBLANES * SUBLANES
                    acc = acc + shifted[q:q + CONV_ROWS, :] * cw_ref[w:w + 1, cs]
            y_ref[pl.ds(base, CONV_ROWS), cs] = acc
        return carry

    lax.fori_loop(0, tm // CONV_ROWS, rows, 0)
    o_ref[...] = _conv_tail(y_ref[...], b_ref, lg_ref, lb_ref, wp_ref)


def _conv_prompt(u3d, cw, cb, lg, lb, wp_bf, *, tm):
    bsz, t, _ = u3d.shape
    halo_per_tile = tm // CONV_HALO
    return pl.pallas_call(
        _conv_prompt_kernel,
        grid=(bsz, t // tm),
        in_specs=[
            pl.BlockSpec((None, tm, CONV_DIM), lambda b, i: (b, i, 0)),
            pl.BlockSpec((None, CONV_HALO, CONV_DIM),
                         lambda b, i: (b, jnp.maximum(i * halo_per_tile - 1, 0), 0)),
            _full((CONV_WIDTH, CONV_DIM)),
            _full((1, CONV_DIM)), _full((1, CONV_DIM)), _full((1, CONV_DIM)),
            _full((CONV_DIM, D_MODEL)),
        ],
        out_specs=pl.BlockSpec((None, tm, D_MODEL), lambda b, i: (b, i, 0)),
        out_shape=jax.ShapeDtypeStruct((bsz, t, D_MODEL), F32),
        scratch_shapes=[pltpu.VMEM((tm + CONV_HALO, CONV_DIM), F32),
                        pltpu.VMEM((tm, CONV_DIM), F32)],
        compiler_params=_params("parallel", "arbitrary"),
        name="conv_prompt",
    )(u3d, u3d, cw, cb, lg, lb, wp_bf)


def _conv_sample_kernel(xp_ref, cw_ref, b_ref, lg_ref, lb_ref, wp_ref, o_ref):
    n_new = o_ref.shape[0]
    for s in range(n_new):
        acc = jnp.zeros(xp_ref.shape[1:], F32)
        for w in range(CONV_WIDTH):
            acc = acc + xp_ref[s + w] * cw_ref[w:w + 1, :]
        o_ref[s] = _conv_tail(acc, b_ref, lg_ref, lb_ref, wp_ref)


def _conv_sample(xp_t, cw, cb, lg, lb, wp_bf):
    rows, bd, _ = xp_t.shape
    n_new = rows - (CONV_WIDTH - 1)
    return pl.pallas_call(
        _conv_sample_kernel,
        grid=(1,),
        in_specs=[_full(xp_t.shape), _full((CONV_WIDTH, CONV_DIM)),
                  _full((1, CONV_DIM)), _full((1, CONV_DIM)), _full((1, CONV_DIM)),
                  _full((CONV_DIM, D_MODEL))],
        out_specs=pl.BlockSpec((n_new, bd, D_MODEL), lambda i: (0, 0, 0)),
        out_shape=jax.ShapeDtypeStruct((n_new, bd, D_MODEL), F32),
        compiler_params=_params("arbitrary"),
        name="conv_sample",
    )(xp_t, cw, cb, lg, lb, wp_bf)


def _attn_prompt_kernel(q_ref, kp_ref, kc_ref, vp_ref, vc_ref, o_ref, l_ref):
    blk = q_ref.shape[0]
    n = pl.program_id(2)
    qi = lax.broadcasted_iota(jnp.int32, (blk, 2 * blk), 0)
    kj = lax.broadcasted_iota(jnp.int32, (blk, 2 * blk), 1)
    first_key = jnp.where(n > 0, qi, blk)
    valid = (kj >= first_key) & (kj <= qi + blk)
    outs, lses = [], []
    for h in range(GROUP_HEADS):
        cs = slice(h * HEAD_DIM, (h + 1) * HEAD_DIM)
        q = q_ref[:, cs]
        k = jnp.concatenate([kp_ref[:, cs], kc_ref[:, cs]], axis=0).astype(BF16)
        v = jnp.concatenate([vp_ref[:, cs], vc_ref[:, cs]], axis=0).astype(BF16)
        s = lax.dot_general(q, k, (((1,), (1,)), ((), ())), preferred_element_type=F32)
        s = jnp.where(valid, s, NEG_INF)
        m = jnp.max(s, axis=-1, keepdims=True)
        p = jnp.exp(s - m)
        l = jnp.sum(p, axis=-1, keepdims=True)
        o = jnp.dot(p.astype(BF16), v, preferred_element_type=F32) / l
        outs.append(o)
        lses.append(jnp.broadcast_to(m + jnp.log(l), (blk, HEAD_DIM)))
    o_ref[...] = jnp.concatenate(outs, axis=1)
    l_ref[...] = jnp.concatenate(lses, axis=1)


def _attn_prompt_group(q3d, k3d, v3d, g):
    bsz, t, _ = q3d.shape
    d = GROUP_DILATIONS[g]
    m = t // d
    blk = WINDOW_STEPS
    assert m % blk == 0
    nb = m // blk
    view = lambda a: a.reshape(bsz, m, d * ATTN_WIDTH)
    cur = lambda b, r, n: (b, n, r * N_GROUPS + g)
    prv = lambda b, r, n: (b, jnp.maximum(n - 1, 0), r * N_GROUPS + g)
    blk_spec = lambda im: pl.BlockSpec((None, blk, GROUP_WIDTH), im)
    o, lse = pl.pallas_call(
        _attn_prompt_kernel,
        grid=(bsz, d, nb),
        in_specs=[blk_spec(cur), blk_spec(prv), blk_spec(cur), blk_spec(prv), blk_spec(cur)],
        out_specs=[pl.BlockSpec((None, blk, GROUP_WIDTH), lambda b, r, n: (b, n, r))] * 2,
        out_shape=[jax.ShapeDtypeStruct((bsz, m, d * GROUP_WIDTH), F32)] * 2,
        compiler_params=_params("parallel", "parallel", "arbitrary"),
        name=f"attn_prompt_g{g + 1}",
    )(view(q3d), view(k3d), view(k3d), view(v3d), view(v3d))
    return o.reshape(bsz, t, GROUP_WIDTH), lse.reshape(bsz, t, GROUP_WIDTH)


def _attn_prompt_residue(qr, kr, vr, g):
    bsz, d, m, _ = qr.shape
    blk = WINDOW_STEPS
    assert d == GROUP_DILATIONS[g] and m % blk == 0
    cur = lambda b, r, n: (b, r, n, 0)
    prv = lambda b, r, n: (b, r, jnp.maximum(n - 1, 0), 0)
    blk_spec = lambda im: pl.BlockSpec((None, None, blk, GROUP_WIDTH), im)
    return pl.pallas_call(
        _attn_prompt_kernel,
        grid=(bsz, d, m // blk),
        in_specs=[blk_spec(cur), blk_spec(prv), blk_spec(cur), blk_spec(prv), blk_spec(cur)],
        out_specs=[blk_spec(cur)] * 2,
        out_shape=[jax.ShapeDtypeStruct((bsz, d, m, GROUP_WIDTH), F32)] * 2,
        compiler_params=_params("parallel", "parallel", "arbitrary"),
        name=f"attn_prompt_g{g + 1}",
    )(qr, kr, kr, vr, vr)


def _combine_kernel(o1, l1, o2, l2, o3, l3, out_ref, scr_ref):
    tm = out_ref.shape[0]

    def natural(x_ref):
        d = x_ref.shape[0]
        for r in range(d):
            for c in range(GROUP_WIDTH // LANES):
                scr_ref[c, pl.ds(r, tm // d, stride=d), :] = x_ref[r, :, c * LANES:(c + 1) * LANES]
        return jnp.concatenate([scr_ref[c] for c in range(GROUP_WIDTH // LANES)], axis=1)

    la, oa = l1[...], o1[...]
    lb, ob = natural(l2), natural(o2)
    lc, oc = natural(l3), natural(o3)
    mx = jnp.maximum(jnp.maximum(la, lb), lc)
    ea, eb, ec = jnp.exp(la - mx), jnp.exp(lb - mx), jnp.exp(lc - mx)
    out_ref[...] = (ea * oa + eb * ob + ec * oc) / (ea + eb + ec)


def _combine_groups(o1, l1, o2, l2, o3, l3, *, tm):
    bsz, t, _ = o1.shape
    nat = pl.BlockSpec((None, tm, GROUP_WIDTH), lambda b, i: (b, i, 0))
    res = lambda d: pl.BlockSpec((None, d, tm // d, GROUP_WIDTH), lambda b, i: (b, 0, i, 0))
    d2, d3 = GROUP_DILATIONS[1:]
    out = pl.pallas_call(
        _combine_kernel,
        grid=(bsz, t // tm),
        in_specs=[nat, nat, res(d2), res(d2), res(d3), res(d3)],
        out_specs=nat,
        out_shape=jax.ShapeDtypeStruct((bsz, t, GROUP_WIDTH), F32),
        scratch_shapes=[pltpu.VMEM((GROUP_WIDTH // LANES, tm, LANES), F32)],
        compiler_params=_params("parallel", "parallel"),
        name="combine_groups",
    )(o1, l1, o2, l2, o3, l3)
    return out.reshape(bsz * t, GROUP_WIDTH)


SAMPLE_ROWS = 8


def _attn_sample_kernel(q_ref, kn_ref, vn_ref, c1_ref, c2_ref, c3_ref, o_ref, *, n_new):
    row = lax.broadcasted_iota(jnp.int32, (GROUP_HEADS, SAMPLE_ROWS, 1), 1)
    stats = []
    for g, c_ref in enumerate((c1_ref, c2_ref, c3_ref)):
        d = GROUP_DILATIONS[g]
        span = c_ref.shape[-1]
        q = q_ref[g]
        k_old = c_ref[0].astype(BF16)
        v_old = c_ref[1].astype(BF16)
        s_old = lax.dot_general(q.astype(BF16), k_old, (((2,), (1,)), ((0,), (0,))),
                                preferred_element_type=F32)
        pos = lax.broadcasted_iota(jnp.int32, (GROUP_HEADS, SAMPLE_ROWS, span), 2)
        ok = (((pos - row) & (d - 1)) == 0) & (pos >= row)
        s_old = jnp.where(ok, s_old, NEG_INF)
        mx = jnp.max(s_old, axis=-1, keepdims=True)
        s_new = []
        for k in range(n_new):
            sk = jnp.sum(q * kn_ref[g, :, k:k + 1, :], axis=-1, keepdims=True)
            ok_new = (row >= k) if g == 0 else (row == k)
            sk = jnp.where(ok_new, sk, NEG_INF)
            s_new.append(sk)
            mx = jnp.maximum(mx, sk)
        p_old = jnp.exp(s_old - mx)
        den = jnp.sum(p_old, axis=-1, keepdims=True)
        acc = lax.dot_general(p_old.astype(BF16), v_old, (((2,), (2,)), ((0,), (0,))),
                              preferred_element_type=F32)
        for k in range(n_new):
            pk = jnp.exp(s_new[k] - mx)
            den = den + pk
            acc = acc + pk * vn_ref[g, :, k:k + 1, :]
        stats.append((mx, den, acc))
    top = jnp.maximum(jnp.maximum(stats[0][0], stats[1][0]), stats[2][0])
    total = jnp.zeros_like(top)
    out = jnp.zeros(o_ref.shape, F32)
    for mx, den, acc in stats:
        w = jnp.exp(mx - top)
        total = total + w * den
        out = out + w * acc
    o_ref[...] = out / total


def _per_head_rows(a2d, bd, n_new):
    a = a2d.reshape(bd, n_new, N_GROUPS, GROUP_HEADS, HEAD_DIM).transpose(0, 2, 3, 1, 4)
    return jnp.pad(a, ((0, 0), (0, 0), (0, 0), (0, SAMPLE_ROWS - n_new), (0, 0)))


def _attn_sample(q2d, k2d, v2d, cache1, cache2, cache3, *, n_new):
    bd = q2d.shape[0] // n_new
    assert n_new <= min(SAMPLE_ROWS, GROUP_DILATIONS[1])
    views = []
    for g, c in enumerate((cache1, cache2, cache3)):
        assert c.shape[1] == GROUP_WINDOWS[g]
        views.append(c.transpose(0, 2, 3, 4, 1))
    small = pl.BlockSpec((None, N_GROUPS, GROUP_HEADS, SAMPLE_ROWS, HEAD_DIM), lambda i: (i, 0, 0, 0, 0))
    cspec = lambda span: pl.BlockSpec((None, 2, GROUP_HEADS, HEAD_DIM, span), lambda i: (i, 0, 0, 0, 0))
    out = pl.pallas_call(
        functools.partial(_attn_sample_kernel, n_new=n_new),
        grid=(bd,),
        in_specs=[small, small, small] + [cspec(w) for w in GROUP_WINDOWS],
        out_specs=pl.BlockSpec((None, GROUP_HEADS, SAMPLE_ROWS, HEAD_DIM), lambda i: (i, 0, 0, 0)),
        out_shape=jax.ShapeDtypeStruct((bd, GROUP_HEADS, SAMPLE_ROWS, HEAD_DIM), F32),
        compiler_params=_params("parallel"),
        name="attn_sample",
    )(*(_per_head_rows(a, bd, n_new) for a in (q2d, k2d, v2d)), *views)
    return out[:, :, :n_new].transpose(0, 2, 1, 3).reshape(bd * n_new, GROUP_WIDTH)


def _finish_kernel(x_ref, heads_ref, conv_ref, gate_ref, wap_ref, wout_ref, n2_ref, wpq_ref, sk_ref,
                   x1_ref, h2_ref, s1t_ref, s2t_ref):
    attn = jnp.dot(heads_ref[...].astype(BF16), wap_ref[...], preferred_element_type=F32)
    mixed = gate_ref[:, :D_MODEL] * conv_ref[...] + gate_ref[:, D_MODEL:] * attn
    x1 = x_ref[...] + jnp.dot(mixed.astype(BF16), wout_ref[...], preferred_element_type=F32)
    x1_ref[...] = x1
    ms = jnp.mean(x1 * x1, axis=-1, keepdims=True)
    h2 = (x1 * lax.rsqrt(ms + RMS_EPS) * n2_ref[...]).astype(BF16)
    h2_ref[...] = h2
    qp = jnp.dot(h2, wpq_ref[...], preferred_element_type=F32).astype(BF16)
    for c in range(PEER_QCOLS // PEER_HALF):
        cs = slice(c * PEER_HALF, (c + 1) * PEER_HALF)
        head, half = divmod(c, 2)
        out_ref = s2t_ref if half else s1t_ref
        scores = lax.dot_general(sk_ref[c], qp[:, cs], (((1,), (1,)), ((), ())), preferred_element_type=F32)
        for lt in range(scores.shape[1] // LANES):
            out_ref[lt, head * N_KEYS:(head + 1) * N_KEYS, :] = scores[:, lt * LANES:(lt + 1) * LANES]


def _finish(x2d, heads, conv_out, gates, wap_bf, wout_bf, n2w, wpq_bf, subk_bf, *, tm):
    n = x2d.shape[0]
    row = lambda i: (i, 0)
    return pl.pallas_call(
        _finish_kernel,
        grid=(n // tm,),
        in_specs=[
            pl.BlockSpec((tm, D_MODEL), row),
            pl.BlockSpec((tm, GROUP_WIDTH), row),
            pl.BlockSpec((tm, D_MODEL), row),
            pl.BlockSpec((tm, 2 * D_MODEL), row),
            _full((GROUP_WIDTH, D_MODEL)),
            _full((D_MODEL, D_MODEL)),
            _full((1, D_MODEL)),
            _full((D_MODEL, PEER_QCOLS)),
            _full((2 * PEER_HEADS, N_KEYS, PEER_HALF)),
        ],
        out_specs=[
            pl.BlockSpec((tm, D_MODEL), row),
            pl.BlockSpec((tm, D_MODEL), row),
            pl.BlockSpec((tm // LANES, PEER_ROWS, LANES), lambda i: (i, 0, 0)),
            pl.BlockSpec((tm // LANES, PEER_ROWS, LANES), lambda i: (i, 0, 0)),
        ],
        out_shape=[
            jax.ShapeDtypeStruct((n, D_MODEL), F32),
            jax.ShapeDtypeStruct((n, D_MODEL), BF16),
            jax.ShapeDtypeStruct((n // LANES, PEER_ROWS, LANES), F32),
            jax.ShapeDtypeStruct((n // LANES, PEER_ROWS, LANES), F32),
        ],
        compiler_params=_params("parallel"),
        name="finish",
    )(x2d, heads, conv_out, gates, wap_bf, wout_bf, n2w, wpq_bf, subk_bf)


def _sorting_network(n):
    def merge(lo, hi, r):
        step = r * 2
        if step < hi - lo:
            yield from merge(lo, hi, step)
            yield from merge(lo + r, hi, step)
            yield from ((i, i + r) for i in range(lo + r, hi - r, step))
        else:
            yield (lo, lo + r)

    def sort(lo, hi):
        if hi > lo:
            mid = lo + (hi - lo) // 2
            yield from sort(lo, mid)
            yield from sort(mid + 1, hi)
            yield from merge(lo, hi, 1)

    return tuple(sort(0, n - 1))


_SORT16 = _sorting_network(PEER_TOPK)
VREGS_PER_KEYSET = N_KEYS // SUBLANES


def _replicated_max(x):
    return jnp.broadcast_to(jnp.max(x, axis=0, keepdims=True), x.shape)


def _replicated_min(x):
    return jnp.broadcast_to(jnp.min(x, axis=0, keepdims=True), x.shape)


def _top16(tiles):
    v = list(tiles)
    for a, b in _SORT16:
        v[a], v[b] = jnp.maximum(v[a], v[b]), jnp.minimum(v[a], v[b])
    tops = []
    for r in range(PEER_TOPK):
        m = _replicated_max(v[0])
        tops.append(m)
        hit = v[0] == m
        for k in range(PEER_TOPK - 1 - r):
            v[k] = jnp.where(hit, v[k + 1], v[k])
    return tops


def _peer_prep_kernel(s1t_ref, s2t_ref, phi_ref, c1_ref, e2_ref):
    sub = lax.broadcasted_iota(jnp.int32, (SUBLANES, LANES), 0)
    neg = jnp.full((SUBLANES, LANES), NEG_INF, F32)
    pos = jnp.full((SUBLANES, LANES), -NEG_INF, F32)

    def stack8(vals):
        out = vals[0]
        for b in range(1, SUBLANES):
            out = jnp.where(sub == b, vals[b], out)
        return out

    def head(h, carry):
        base = pl.multiple_of(h * N_KEYS, N_KEYS)
        rows = [pl.ds(base + k * SUBLANES, SUBLANES) for k in range(VREGS_PER_KEYSET)]
        s1 = [s1t_ref[r, :] for r in rows]
        s2 = [s2t_ref[r, :] for r in rows]
        top1 = _top16(s1)
        top2 = _top16(s2)
        a_lo, a_hi = stack8(top1[:SUBLANES]), stack8(top1[SUBLANES:])
        b_lo, b_hi = stack8(top2[:SUBLANES]), stack8(top2[SUBLANES:])
        cands = [top1[0] + b_lo, top1[0] + b_hi, top1[1] + b_lo, top1[2] + b_lo, top1[3] + b_lo]
        seen = sub < 4
        cands += [jnp.where(seen, neg, a_lo + top2[0]), a_hi + top2[0]]
        cands += [jnp.where(seen, neg, a_lo + top2[b]) for b in (1, 2, 3)]
        tops = _top16(cands + [neg] * (PEER_TOPK - len(cands)))
        best, tau = tops[0], tops[-1]
        z = jnp.zeros((SUBLANES, LANES), F32)
        for c in cands:
            z = z + jnp.where(c >= tau, jnp.exp(c - best), 0.0)
        inv_z = 1.0 / jnp.broadcast_to(jnp.sum(z, axis=0, keepdims=True), z.shape)
        phis = []
        for a in range(PEER_TOPK):
            t = jnp.where(top1[a] + b_lo >= tau, b_lo, pos)
            if a == 0:
                t = jnp.minimum(t, jnp.where(top1[a] + b_hi >= tau, b_hi, pos))
            phis.append(_replicated_min(t))
        for k, r in enumerate(rows):
            phi = pos
            for a in range(PEER_TOPK):
                phi = jnp.where(s1[k] == top1[a], phis[a], phi)
            phi_ref[r, :] = phi
            c1_ref[r, :] = jnp.exp(s1[k] - top1[0]) * inv_z
            e2_ref[r, :] = jnp.exp(s2[k] - top2[0])
        return carry

    def head_pair(hp, carry):
        head(2 * hp, carry)
        head(2 * hp + 1, carry)
        return carry

    lax.fori_loop(0, PEER_HEADS // 2, head_pair, 0)


def _peer_prep(s1t, s2t):
    slabs = s1t.shape[0]
    spec = pl.BlockSpec((None, PEER_ROWS, LANES), lambda i: (i, 0, 0))
    return pl.pallas_call(
        _peer_prep_kernel,
        grid=(slabs,),
        in_specs=[spec, spec],
        out_specs=[spec] * 3,
        out_shape=[jax.ShapeDtypeStruct((slabs, PEER_ROWS, LANES), F32)] * 3,
        compiler_params=_params("parallel"),
        name="peer_prep",
    )(s1t, s2t)


def _gelu(a):
    return 0.5 * a * (1.0 + lax.erf(a * np.float32(1.0 / np.sqrt(2.0))))


def _peer_dense_kernel(h2_ref, u_ref, vt_ref, phi_ref, c1_ref, s2_ref, e2_ref, x1_ref,
                       y_ref, acc_ref, at_ref, wg_ref, *, keys_per_step):
    j = pl.program_id(1)
    tn = h2_ref.shape[0]

    @pl.when(j == 0)
    def _():
        acc_ref[...] = jnp.zeros_like(acc_ref)

    at_ref[...] = lax.dot_general(u_ref[...], h2_ref[...], (((1,), (1,)), ((), ())),
                                  preferred_element_type=F32)

    def first_key(ii, carry):
        i = j * keys_per_step + ii
        out0 = pl.multiple_of(ii * N_KEYS, N_KEYS)
        for lt in range(tn // LANES):
            lanes = slice(lt * LANES, (lt + 1) * LANES)
            w = [None] * VREGS_PER_KEYSET
            for h in range(PEER_HEADS):
                row_i = pl.ds(h * N_KEYS + i, 1)
                phi = jnp.broadcast_to(phi_ref[lt, row_i, :], (SUBLANES, LANES))
                c1 = jnp.broadcast_to(c1_ref[lt, row_i, :], (SUBLANES, LANES))
                for kb in range(VREGS_PER_KEYSET):
                    rows = slice(h * N_KEYS + kb * SUBLANES, h * N_KEYS + (kb + 1) * SUBLANES)
                    t = jnp.where(s2_ref[lt, rows, :] >= phi, e2_ref[lt, rows, :], 0.0) * c1
                    w[kb] = t if h == 0 else w[kb] + t
            for kp in range(VREGS_PER_KEYSET // 2):
                rows = pl.ds(out0 + kp * 2 * SUBLANES, 2 * SUBLANES)
                pair = jnp.concatenate([w[2 * kp], w[2 * kp + 1]], axis=0)
                wg_ref[rows, lanes] = (pair * _gelu(at_ref[rows, lanes])).astype(BF16)
        return carry

    lax.fori_loop(0, keys_per_step, first_key, 0)
    acc_ref[...] += jnp.dot(vt_ref[...], wg_ref[...], preferred_element_type=F32)

    @pl.when(j == pl.num_programs(1) - 1)
    def _():
        y_ref[...] = x1_ref[...] + acc_ref[...].T


def _peer_dense(h2, eu_bf, evt_bf, phi, c1, s2t, e2, x1, *, tn, keys_per_step):
    n = h2.shape[0]
    ec = keys_per_step * N_KEYS
    tok = lambda i, j: (i, 0)
    slab = pl.BlockSpec((tn // LANES, PEER_ROWS, LANES), lambda i, j: (i, 0, 0))
    return pl.pallas_call(
        functools.partial(_peer_dense_kernel, keys_per_step=keys_per_step),
        grid=(n // tn, N_EXPERTS // ec),
        in_specs=[
            pl.BlockSpec((tn, D_MODEL), tok),
            pl.BlockSpec((ec, D_MODEL), lambda i, j: (j, 0)),
            pl.BlockSpec((D_MODEL, ec), lambda i, j: (0, j)),
            slab, slab, slab, slab,
            pl.BlockSpec((tn, D_MODEL), tok),
        ],
        out_specs=pl.BlockSpec((tn, D_MODEL), tok),
        out_shape=jax.ShapeDtypeStruct((n, D_MODEL), F32),
        scratch_shapes=[pltpu.VMEM((D_MODEL, tn), F32),
                        pltpu.VMEM((ec, tn), F32),
                        pltpu.VMEM((ec, tn), BF16)],
        compiler_params=_params("parallel", "arbitrary"),
        name="peer_dense",
    )(h2, eu_bf, evt_bf, phi, c1, s2t, e2, x1)


def _token_tail(x2d, heads, conv_out, gates, lw, *, tm, tn, keys_per_step):
    x1, h2, s1t, s2t = _finish(x2d, heads, conv_out, gates, lw["wap"], lw["wout"], lw["n2"], lw["wpq"],
                               lw["subk"], tm=tm)
    phi, c1, e2 = _peer_prep(s1t, s2t)
    return _peer_dense(h2, lw["eu"], lw["evt"], phi, c1, s2t, e2, x1, tn=tn, keys_per_step=keys_per_step)


def kernel(x_prompt, x_sample, cache_kv_g1, cache_kv_g2, cache_kv_g3, state_conv,
           norm1_w, w_in, q_norm_w, k_norm_w, conv_w, conv_b, conv_ln_g, conv_ln_b,
           w_conv_proj, w_attn_proj, w_out, norm2_w, w_peer_q, peer_sub_keys,
           expert_u, expert_v):
    depth = w_in.shape[0]
    bsz, seq, _ = x_prompt.shape
    bd, n_new, _ = x_sample.shape
    past_len = cache_kv_g3.shape[2]
    caches = (cache_kv_g1, cache_kv_g2, cache_kv_g3)

    seg = _head_mean_matrix()
    cos_p, sin_p = _rope_tables(np.arange(seq))
    cos_s, sin_s = _rope_tables(past_len + np.arange(bd * n_new) % n_new)
    q_scale = HEAD_DIM ** -0.5

    hp = x_prompt.reshape(bsz * seq, D_MODEL)
    hs = x_sample.reshape(bd * n_new, D_MODEL)
    kv_p = [[], [], []]
    kv_s = [[], [], []]
    conv_p, conv_s = [], []
    for layer in range(depth):
        tile = lambda w: jnp.tile(w[layer], GROUP_HEADS)[None, :]
        lw = dict(
            wap=w_attn_proj[layer].astype(BF16), wout=w_out[layer].astype(BF16),
            n2=norm2_w[layer][None, :], wpq=w_peer_q[layer].astype(BF16),
            subk=peer_sub_keys[layer].reshape(2 * PEER_HEADS, N_KEYS, PEER_HALF).astype(BF16),
            eu=expert_u[layer].astype(BF16), evt=expert_v[layer].T.astype(BF16))
        w_in_bf = w_in[layer].astype(BF16)
        n1 = norm1_w[layer][None, :]
        qnw, knw = tile(q_norm_w), tile(k_norm_w)
        conv_args = (conv_w[layer], conv_b[layer][None, :], conv_ln_g[layer][None, :],
                     conv_ln_b[layer][None, :], w_conv_proj[layer].astype(BF16))

        q, k, v, u, gates, *by_res = _mixer_in(hp, seq, cos_p, sin_p, n1, w_in_bf, qnw, knw, seg,
                                               tm=256, q_dtype=BF16, q_scale=q_scale, by_residue=True)
        u3 = u.reshape(bsz, seq, CONV_DIM)
        c_out = _conv_prompt(u3, *conv_args, tm=512).reshape(bsz * seq, D_MODEL)
        q3, k3, v3 = (a.reshape(bsz, seq, ATTN_WIDTH) for a in (q, k, v))
        o1, l1 = _attn_prompt_group(q3, k3, v3, 0)
        o2, l2 = _attn_prompt_residue(*by_res[0:3], 1)
        o3, l3 = _attn_prompt_residue(*by_res[3:6], 2)
        heads = _combine_groups(o1, l1, o2, l2, o3, l3, tm=512)
        hp = _token_tail(hp, heads, c_out, gates, lw, tm=512, tn=512, keys_per_step=16)
        conv_p.append(u3[:, seq - (CONV_WIDTH - 1):])
        for g in range(N_GROUPS):
            keep = min(GROUP_WINDOWS[g], seq)
            gs = slice(g * GROUP_WIDTH, (g + 1) * GROUP_WIDTH)
            kv = jnp.stack([k3[:, seq - keep:, gs], v3[:, seq - keep:, gs]], axis=2)
            kv_p[g].append(kv.reshape(bsz, keep, 2, GROUP_HEADS, HEAD_DIM))

        q, k, v, u, gates = _mixer_in(hs, bd * n_new, cos_s, sin_s, n1, w_in_bf, qnw, knw, seg,
                                      tm=bd * n_new, q_dtype=F32, q_scale=q_scale)
        u3 = u.reshape(bd, n_new, CONV_DIM)
        xp = jnp.concatenate([state_conv[layer], u3], axis=1)
        c_out = _conv_sample(xp.transpose(1, 0, 2), *conv_args)
        c_out = c_out.transpose(1, 0, 2).reshape(bd * n_new, D_MODEL)
        heads = _attn_sample(q, k, v, *(c[layer] for c in caches), n_new=n_new)
        hs = _token_tail(hs, heads, c_out, gates, lw, tm=bd * n_new, tn=bd * n_new, keys_per_step=16)
        conv_s.append(xp[:, n_new:])
        k3, v3 = (a.reshape(bd, n_new, ATTN_WIDTH) for a in (k, v))
        for g in range(N_GROUPS):
            gs = slice(g * GROUP_WIDTH, (g + 1) * GROUP_WIDTH)
            kv = jnp.stack([k3[:, :, gs], v3[:, :, gs]], axis=2)
            kv_s[g].append(kv.reshape(bd, n_new, 2, GROUP_HEADS, HEAD_DIM))

    return (hp.reshape(bsz, seq, D_MODEL), hs.reshape(bd, n_new, D_MODEL),
            jnp.stack(kv_p[0]), jnp.stack(kv_p[1]), jnp.stack(kv_p[2]), jnp.stack(conv_p),
            jnp.stack(kv_s[0]), jnp.stack(kv_s[1]), jnp.stack(kv_s[2]), jnp.stack(conv_s))
```

```python
import functools

import numpy as np
import jax
import jax.numpy as jnp
from jax import lax
from jax.experimental import pallas as pl
from jax.experimental.pallas import tpu as pltpu

F32 = jnp.float32
BF16 = jnp.bfloat16

D_MODEL = 1024
HEAD_DIM = 64
HALF_DIM = HEAD_DIM // 2
GROUP_HEADS = 8
N_GROUPS = 3
GROUP_WIDTH = GROUP_HEADS * HEAD_DIM
ATTN_WIDTH = N_GROUPS * GROUP_WIDTH
GROUP_WINDOWS = (128, 512, 2048)
GROUP_DILATIONS = (1, 4, 16)
WINDOW_STEPS = 128
ROPE_THETA = 10000.0
CONV_DIM = D_MODEL
CONV_WIDTH = 31
CONV_HALO = 32
PEER_HEADS = 8
N_KEYS = 128
N_EXPERTS = N_KEYS * N_KEYS
PEER_TOPK = 16
PEER_HALF = 128
PEER_QCOLS = PEER_HEADS * 2 * PEER_HALF
PEER_ROWS = PEER_HEADS * N_KEYS
LANES = 128
SUBLANES = 8
RMS_EPS = 1e-6
LN_EPS = 1e-5
NEG_INF = float("-inf")

VMEM_LIMIT = 56 * 1024 * 1024


def _params(*sem):
    return pltpu.CompilerParams(dimension_semantics=sem, vmem_limit_bytes=VMEM_LIMIT)


def _full(shape):
    return pl.BlockSpec(shape, lambda *_: (0,) * len(shape), pipeline_mode=pl.Buffered(1))


def _split_bf16(x):
    hi = x.astype(BF16)
    lo = (x - hi.astype(F32)).astype(BF16)
    return hi, lo


def _rope_tables(positions):
    inv_freq = ROPE_THETA ** (-np.arange(HALF_DIM, dtype=np.float64) / HALF_DIM)
    ang = np.asarray(positions, np.float64)[:, None] * inv_freq[None, :]
    cos = np.cos(ang)
    sin = np.sin(ang)
    cos_t = np.concatenate([cos, cos, cos, cos], axis=1)
    sin_t = np.concatenate([-sin, sin, -sin, sin], axis=1)
    return jnp.asarray(cos_t, F32), jnp.asarray(sin_t, F32)


def _head_mean_matrix():
    idx = np.arange(GROUP_WIDTH) // HEAD_DIM
    return jnp.asarray((idx[:, None] == idx[None, :]) / HEAD_DIM, BF16)


def _mixer_in_kernel(x_ref, n1_ref, w_ref, cos_ref, sin_ref, qn_ref, kn_ref, seg_ref,
                     q_ref, k_ref, v_ref, u_ref, g_ref, *rest, q_scale):
    tm = x_ref.shape[0]
    x = x_ref[...]
    ms = jnp.mean(x * x, axis=-1, keepdims=True)
    h = (x * lax.rsqrt(ms + RMS_EPS) * n1_ref[...]).astype(BF16)
    cos = jnp.concatenate([cos_ref[...]] * 4, axis=1)
    sin = jnp.concatenate([sin_ref[...]] * 4, axis=1)
    lane = lax.broadcasted_iota(jnp.int32, (tm, GROUP_WIDTH), 1)
    first_half = (lane % HEAD_DIM) < HALF_DIM

    def proj(c0, width):
        return jnp.dot(h, w_ref[:, c0:c0 + width], preferred_element_type=F32)

    def normed_rotated(c0, nw_ref):
        z = proj(c0, GROUP_WIDTH)
        msq = jnp.dot((z * z).astype(BF16), seg_ref[...], preferred_element_type=F32)
        zn = z * lax.rsqrt(msq + RMS_EPS) * nw_ref[...]
        swapped = jnp.where(first_half,
                            pltpu.roll(zn, GROUP_WIDTH - HALF_DIM, 1),
                            pltpu.roll(zn, HALF_DIM, 1))
        return zn * cos + swapped * sin

    def by_residue(val, out_ref, scr_ref):
        d = out_ref.shape[0]
        for c in range(GROUP_WIDTH // LANES):
            scr_ref[c] = val[:, c * LANES:(c + 1) * LANES]
        for r in range(d):
            for c in range(GROUP_WIDTH // LANES):
                out_ref[r, :, c * LANES:(c + 1) * LANES] = (
                    scr_ref[c, pl.ds(r, tm // d, stride=d), :].astype(out_ref.dtype))

    for g in range(N_GROUPS):
        cs = slice(g * GROUP_WIDTH, (g + 1) * GROUP_WIDTH)
        qg = normed_rotated(g * GROUP_WIDTH, qn_ref) * q_scale
        kg = normed_rotated(ATTN_WIDTH + g * GROUP_WIDTH, kn_ref)
        vg = proj(2 * ATTN_WIDTH + g * GROUP_WIDTH, GROUP_WIDTH)
        q_ref[:, cs] = qg.astype(q_ref.dtype)
        k_ref[:, cs] = kg
        v_ref[:, cs] = vg
        if rest and g > 0:
            for val, out_ref in zip((qg, kg, vg), rest[3 * (g - 1):3 * g]):
                by_residue(val, out_ref, rest[-1])
    glu0 = 3 * ATTN_WIDTH
    for c in range(CONV_DIM // GROUP_WIDTH):
        cs = slice(c * GROUP_WIDTH, (c + 1) * GROUP_WIDTH)
        a = proj(glu0 + c * GROUP_WIDTH, GROUP_WIDTH)
        b = proj(glu0 + CONV_DIM + c * GROUP_WIDTH, GROUP_WIDTH)
        u_ref[:, cs] = a * jax.nn.sigmoid(b)
    gate0 = glu0 + 2 * CONV_DIM
    for c in range(2 * D_MODEL // GROUP_WIDTH):
        cs = slice(c * GROUP_WIDTH, (c + 1) * GROUP_WIDTH)
        g_ref[:, cs] = jax.nn.sigmoid(proj(gate0 + c * GROUP_WIDTH, GROUP_WIDTH))


def _mixer_in(x2d, positions_period, cos_t, sin_t, n1w, w_in_bf, qnw, knw, seg, *, tm, q_dtype, q_scale,
              by_residue=False):
    n = x2d.shape[0]
    in_cols = w_in_bf.shape[1]
    period_tiles = positions_period // tm
    row = lambda i: (i, 0)
    extra_specs, extra_shapes, scratch = [], [], []
    if by_residue:
        for d in GROUP_DILATIONS[1:]:
            spec = pl.BlockSpec((None, d, tm // d, GROUP_WIDTH),
                                lambda i: (i // period_tiles, 0, i % period_tiles, 0))
            shape = jax.ShapeDtypeStruct((n // positions_period, d, positions_period // d, GROUP_WIDTH), BF16)
            extra_specs += [spec] * 3
            extra_shapes += [shape] * 3
        scratch = [pltpu.VMEM((GROUP_WIDTH // LANES, tm, LANES), F32)]
    return pl.pallas_call(
        functools.partial(_mixer_in_kernel, q_scale=q_scale),
        grid=(n // tm,),
        in_specs=[
            pl.BlockSpec((tm, D_MODEL), row),
            _full((1, D_MODEL)),
            _full((D_MODEL, in_cols)),
            pl.BlockSpec((tm, 128), lambda i: (i % period_tiles, 0)),
            pl.BlockSpec((tm, 128), lambda i: (i % period_tiles, 0)),
            _full((1, GROUP_WIDTH)),
            _full((1, GROUP_WIDTH)),
            _full((GROUP_WIDTH, GROUP_WIDTH)),
        ],
        out_specs=[
            pl.BlockSpec((tm, ATTN_WIDTH), row),
            pl.BlockSpec((tm, ATTN_WIDTH), row),
            pl.BlockSpec((tm, ATTN_WIDTH), row),
            pl.BlockSpec((tm, CONV_DIM), row),
            pl.BlockSpec((tm, 2 * D_MODEL), row),
        ] + extra_specs,
        out_shape=[
            jax.ShapeDtypeStruct((n, ATTN_WIDTH), q_dtype),
            jax.ShapeDtypeStruct((n, ATTN_WIDTH), F32),
            jax.ShapeDtypeStruct((n, ATTN_WIDTH), F32),
            jax.ShapeDtypeStruct((n, CONV_DIM), F32),
            jax.ShapeDtypeStruct((n, 2 * D_MODEL), F32),
        ] + extra_shapes,
        scratch_shapes=scratch,
        compiler_params=_params("parallel"),
        name="mixer_in",
    )(x2d, n1w, w_in_bf, cos_t, sin_t, qnw, knw, seg)


CONV_ROWS = 32
CONV_LANES = 256


def _conv_tail(y, b_ref, lg_ref, lb_ref, wp_ref):
    y = y + b_ref[...]
    mu = jnp.mean(y, axis=-1, keepdims=True)
    yc = y - mu
    var = jnp.mean(yc * yc, axis=-1, keepdims=True)
    yn = yc * lax.rsqrt(var + LN_EPS) * lg_ref[...] + lb_ref[...]
    act = yn * jax.nn.sigmoid(yn)
    return jnp.dot(act.astype(BF16), wp_ref[...], preferred_element_type=F32)


def _conv_prompt_kernel(u_ref, prev_ref, cw_ref, b_ref, lg_ref, lb_ref, wp_ref, o_ref, xs_ref, y_ref):
    tm = u_ref.shape[0]
    i = pl.program_id(1)
    prev = prev_ref[...]
    xs_ref[0:CONV_HALO, :] = jnp.where(i == 0, jnp.zeros_like(prev), prev)
    xs_ref[CONV_HALO:, :] = u_ref[...]
    first = CONV_HALO - (CONV_WIDTH - 1)

    def rows(r, carry):
        base = pl.multiple_of(r * CONV_ROWS, CONV_ROWS)
        for c in range(CONV_DIM // CONV_LANES):
            cs = slice(c * CONV_LANES, (c + 1) * CONV_LANES)
            window = xs_ref[pl.ds(base, CONV_ROWS + CONV_HALO), cs]
            acc = jnp.zeros((CONV_ROWS, CONV_LANES), F32)
            for r in range(SUBLANES):
                taps = [w for w in range(CONV_WIDTH) if (first + w) % SUBLANES == r]
                shifted = window if r == 0 else pltpu.roll(window, CONV_ROWS + CONV_HALO - r, 0)
                for w in taps:
                    q = (first + w) // SUBLANES * SUBLANES
                    acc = acc + shifted[q:q + CONV_ROWS, :] * cw_ref[w:w + 1, cs]
            y_ref[pl.ds(base, CONV_ROWS), cs] = acc
        return carry

    lax.fori_loop(0, tm // CONV_ROWS, rows, 0)
    o_ref[...] = _conv_tail(y_ref[...], b_ref, lg_ref, lb_ref, wp_ref)


def _conv_prompt(u3d, cw, cb, lg, lb, wp_bf, *, tm):
    bsz, t, _ = u3d.shape
    halo_per_tile = tm // CONV_HALO
    return pl.pallas_call(
        _conv_prompt_kernel,
        grid=(bsz, t // tm),
        in_specs=[
            pl.BlockSpec((None, tm, CONV_DIM), lambda b, i: (b, i, 0)),
            pl.BlockSpec((None, CONV_HALO, CONV_DIM),
                         lambda b, i: (b, jnp.maximum(i * halo_per_tile - 1, 0), 0)),
            _full((CONV_WIDTH, CONV_DIM)),
            _full((1, CONV_DIM)), _full((1, CONV_DIM)), _full((1, CONV_DIM)),
            _full((CONV_DIM, D_MODEL)),
        ],
        out_specs=pl.BlockSpec((None, tm, D_MODEL), lambda b, i: (b, i, 0)),
        out_shape=jax.ShapeDtypeStruct((bsz, t, D_MODEL), F32),
        scratch_shapes=[pltpu.VMEM((tm + CONV_HALO, CONV_DIM), F32),
                        pltpu.VMEM((tm, CONV_DIM), F32)],
        compiler_params=_params("parallel", "arbitrary"),
        name="conv_prompt",
    )(u3d, u3d, cw, cb, lg, lb, wp_bf)


def _conv_sample_kernel(xp_ref, cw_ref, b_ref, lg_ref, lb_ref, wp_ref, o_ref):
    n_new = o_ref.shape[0]
    for s in range(n_new):
        acc = jnp.zeros(xp_ref.shape[1:], F32)
        for w in range(CONV_WIDTH):
            acc = acc + xp_ref[s + w] * cw_ref[w:w + 1, :]
        o_ref[s] = _conv_tail(acc, b_ref, lg_ref, lb_ref, wp_ref)


def _conv_sample(xp_t, cw, cb, lg, lb, wp_bf):
    rows, bd, _ = xp_t.shape
    n_new = rows - (CONV_WIDTH - 1)
    return pl.pallas_call(
        _conv_sample_kernel,
        grid=(1,),
        in_specs=[_full(xp_t.shape), _full((CONV_WIDTH, CONV_DIM)),
                  _full((1, CONV_DIM)), _full((1, CONV_DIM)), _full((1, CONV_DIM)),
                  _full((CONV_DIM, D_MODEL))],
        out_specs=pl.BlockSpec((n_new, bd, D_MODEL), lambda i: (0, 0, 0)),
        out_shape=jax.ShapeDtypeStruct((n_new, bd, D_MODEL), F32),
        compiler_params=_params("arbitrary"),
        name="conv_sample",
    )(xp_t, cw, cb, lg, lb, wp_bf)


def _attn_prompt_kernel(q_ref, kp_ref, kc_ref, vp_ref, vc_ref, o_ref, l_ref):
    blk = q_ref.shape[0]
    n = pl.program_id(2)
    qi = lax.broadcasted_iota(jnp.int32, (blk, 2 * blk), 0)
    kj = lax.broadcasted_iota(jnp.int32, (blk, 2 * blk), 1)
    first_key = jnp.where(n > 0, qi, blk)
    valid = (kj >= first_key) & (kj <= qi + blk)
    outs, lses = [], []
    for h in range(GROUP_HEADS):
        cs = slice(h * HEAD_DIM, (h + 1) * HEAD_DIM)
        q = q_ref[:, cs]
        k = jnp.concatenate([kp_ref[:, cs], kc_ref[:, cs]], axis=0).astype(BF16)
        v = jnp.concatenate([vp_ref[:, cs], vc_ref[:, cs]], axis=0).astype(BF16)
        s = lax.dot_general(q, k, (((1,), (1,)), ((), ())), preferred_element_type=F32)
        s = jnp.where(valid, s, NEG_INF)
        m = jnp.max(s, axis=-1, keepdims=True)
        p = jnp.exp(s - m)
        l = jnp.sum(p, axis=-1, keepdims=True)
        o = jnp.dot(p.astype(BF16), v, preferred_element_type=F32) / l
        outs.append(o)
        lses.append(jnp.broadcast_to(m + jnp.log(l), (blk, HEAD_DIM)))
    o_ref[...] = jnp.concatenate(outs, axis=1)
    l_ref[...] = jnp.concatenate(lses, axis=1)


def _attn_prompt_group(q3d, k3d, v3d, g):
    bsz, t, _ = q3d.shape
    d = GROUP_DILATIONS[g]
    m = t // d
    blk = WINDOW_STEPS
    assert m % blk == 0
    nb = m // blk
    view = lambda a: a.reshape(bsz, m, d * ATTN_WIDTH)
    cur = lambda b, r, n: (b, n, r * N_GROUPS + g)
    prv = lambda b, r, n: (b, jnp.maximum(n - 1, 0), r * N_GROUPS + g)
    blk_spec = lambda im: pl.BlockSpec((None, blk, GROUP_WIDTH), im)
    o, lse = pl.pallas_call(
        _attn_prompt_kernel,
        grid=(bsz, d, nb),
        in_specs=[blk_spec(cur), blk_spec(prv), blk_spec(cur), blk_spec(prv), blk_spec(cur)],
        out_specs=[pl.BlockSpec((None, blk, GROUP_WIDTH), lambda b, r, n: (b, n, r))] * 2,
        out_shape=[jax.ShapeDtypeStruct((bsz, m, d * GROUP_WIDTH), F32)] * 2,
        compiler_params=_params("parallel", "parallel", "arbitrary"),
        name=f"attn_prompt_g{g + 1}",
    )(view(q3d), view(k3d), view(k3d), view(v3d), view(v3d))
    return o.reshape(bsz, t, GROUP_WIDTH), lse.reshape(bsz, t, GROUP_WIDTH)


def _attn_prompt_residue(qr, kr, vr, g):
    bsz, d, m, _ = qr.shape
    blk = WINDOW_STEPS
    assert d == GROUP_DILATIONS[g] and m % blk == 0
    cur = lambda b, r, n: (b, r, n, 0)
    prv = lambda b, r, n: (b, r, jnp.maximum(n - 1, 0), 0)
    blk_spec = lambda im: pl.BlockSpec((None, None, blk, GROUP_WIDTH), im)
    return pl.pallas_call(
        _attn_prompt_kernel,
        grid=(bsz, d, m // blk),
        in_specs=[blk_spec(cur), blk_spec(prv), blk_spec(cur), blk_spec(prv), blk_spec(cur)],
        out_specs=[blk_spec(cur)] * 2,
        out_shape=[jax.ShapeDtypeStruct((bsz, d, m, GROUP_WIDTH), F32)] * 2,
        compiler_params=_params("parallel", "parallel", "arbitrary"),
        name=f"attn_prompt_g{g + 1}",
    )(qr, kr, kr, vr, vr)


def _combine_kernel(o1, l1, o2, l2, o3, l3, out_ref, scr_ref):
    tm = out_ref.shape[0]

    def natural(x_ref):
        d = x_ref.shape[0]
        for r in range(d):
            for c in range(GROUP_WIDTH // LANES):
                scr_ref[c, pl.ds(r, tm // d, stride=d), :] = x_ref[r, :, c * LANES:(c + 1) * LANES]
        return jnp.concatenate([scr_ref[c] for c in range(GROUP_WIDTH // LANES)], axis=1)

    la, oa = l1[...], o1[...]
    lb, ob = natural(l2), natural(o2)
    lc, oc = natural(l3), natural(o3)
    mx = jnp.maximum(jnp.maximum(la, lb), lc)
    ea, eb, ec = jnp.exp(la - mx), jnp.exp(lb - mx), jnp.exp(lc - mx)
    out_ref[...] = (ea * oa + eb * ob + ec * oc) / (ea + eb + ec)


def _combine_groups(o1, l1, o2, l2, o3, l3, *, tm):
    bsz, t, _ = o1.shape
    nat = pl.BlockSpec((None, tm, GROUP_WIDTH), lambda b, i: (b, i, 0))
    res = lambda d: pl.BlockSpec((None, d, tm // d, GROUP_WIDTH), lambda b, i: (b, 0, i, 0))
    d2, d3 = GROUP_DILATIONS[1:]
    out = pl.pallas_call(
        _combine_kernel,
        grid=(bsz, t // tm),
        in_specs=[nat, nat, res(d2), res(d2), res(d3), res(d3)],
        out_specs=nat,
        out_shape=jax.ShapeDtypeStruct((bsz, t, GROUP_WIDTH), F32),
        scratch_shapes=[pltpu.VMEM((GROUP_WIDTH // LANES, tm, LANES), F32)],
        compiler_params=_params("parallel", "parallel"),
        name="combine_groups",
    )(o1, l1, o2, l2, o3, l3)
    return out.reshape(bsz * t, GROUP_WIDTH)


SAMPLE_ROWS = 8


def _attn_sample_kernel(q_ref, kn_ref, vn_ref, c1_ref, c2_ref, c3_ref, o_ref, *, n_new):
    row = lax.broadcasted_iota(jnp.int32, (GROUP_HEADS, SAMPLE_ROWS, 1), 1)
    stats = []
    for g, c_ref in enumerate((c1_ref, c2_ref, c3_ref)):
        d = GROUP_DILATIONS[g]
        span = c_ref.shape[-1]
        q = q_ref[g]
        k_old = c_ref[0].astype(BF16)
        v_old = c_ref[1].astype(BF16)
        s_old = lax.dot_general(q.astype(BF16), k_old, (((2,), (1,)), ((0,), (0,))),
                                preferred_element_type=F32)
        pos = lax.broadcasted_iota(jnp.int32, (GROUP_HEADS, SAMPLE_ROWS, span), 2)
        ok = (((pos - row) & (d - 1)) == 0) & (pos >= row)
        s_old = jnp.where(ok, s_old, NEG_INF)
        mx = jnp.max(s_old, axis=-1, keepdims=True)
        s_new = []
        for k in range(n_new):
            sk = jnp.sum(q * kn_ref[g, :, k:k + 1, :], axis=-1, keepdims=True)
            ok_new = (row >= k) if g == 0 else (row == k)
            sk = jnp.where(ok_new, sk, NEG_INF)
            s_new.append(sk)
            mx = jnp.maximum(mx, sk)
        p_old = jnp.exp(s_old - mx)
        den = jnp.sum(p_old, axis=-1, keepdims=True)
        acc = lax.dot_general(p_old.astype(BF16), v_old, (((2,), (2,)), ((0,), (0,))),
                              preferred_element_type=F32)
        for k in range(n_new):
            pk = jnp.exp(s_new[k] - mx)
            den = den + pk
            acc = acc + pk * vn_ref[g, :, k:k + 1, :]
        stats.append((mx, den, acc))
    top = jnp.maximum(jnp.maximum(stats[0][0], stats[1][0]), stats[2][0])
    total = jnp.zeros_like(top)
    out = jnp.zeros(o_ref.shape, F32)
    for mx, den, acc in stats:
        w = jnp.exp(mx - top)
        total = total + w * den
        out = out + w * acc
    o_ref[...] = out / total


def _per_head_rows(a2d, bd, n_new):
    a = a2d.reshape(bd, n_new, N_GROUPS, GROUP_HEADS, HEAD_DIM).transpose(0, 2, 3, 1, 4)
    return jnp.pad(a, ((0, 0), (0, 0), (0, 0), (0, SAMPLE_ROWS - n_new), (0, 0)))


def _attn_sample(q2d, k2d, v2d, cache1, cache2, cache3, *, n_new):
    bd = q2d.shape[0] // n_new
    assert n_new <= min(SAMPLE_ROWS, GROUP_DILATIONS[1])
    views = []
    for g, c in enumerate((cache1, cache2, cache3)):
        assert c.shape[1] == GROUP_WINDOWS[g]
        views.append(c.transpose(0, 2, 3, 4, 1))
    small = pl.BlockSpec((None, N_GROUPS, GROUP_HEADS, SAMPLE_ROWS, HEAD_DIM), lambda i: (i, 0, 0, 0, 0))
    cspec = lambda span: pl.BlockSpec((None, 2, GROUP_HEADS, HEAD_DIM, span), lambda i: (i, 0, 0, 0, 0))
    out = pl.pallas_call(
        functools.partial(_attn_sample_kernel, n_new=n_new),
        grid=(bd,),
        in_specs=[small, small, small] + [cspec(w) for w in GROUP_WINDOWS],
        out_specs=pl.BlockSpec((None, GROUP_HEADS, SAMPLE_ROWS, HEAD_DIM), lambda i: (i, 0, 0, 0)),
        out_shape=jax.ShapeDtypeStruct((bd, GROUP_HEADS, SAMPLE_ROWS, HEAD_DIM), F32),
        compiler_params=_params("parallel"),
        name="attn_sample",
    )(*(_per_head_rows(a, bd, n_new) for a in (q2d, k2d, v2d)), *views)
    return out[:, :, :n_new].transpose(0, 2, 1, 3).reshape(bd * n_new, GROUP_WIDTH)


def _finish_kernel(x_ref, heads_ref, conv_ref, gate_ref, wap_ref, wout_ref, n2_ref, wpq_ref, sk_ref,
                   x1_ref, h2t_ref, s1t_ref, s2t_ref):
    attn = jnp.dot(heads_ref[...].astype(BF16), wap_ref[...], preferred_element_type=F32)
    mixed = gate_ref[:, :D_MODEL] * conv_ref[...] + gate_ref[:, D_MODEL:] * attn
    x1 = x_ref[...] + jnp.dot(mixed.astype(BF16), wout_ref[...], preferred_element_type=F32)
    x1_ref[...] = x1
    ms = jnp.mean(x1 * x1, axis=-1, keepdims=True)
    h2f = x1 * lax.rsqrt(ms + RMS_EPS) * n2_ref[...]
    h2 = h2f.astype(BF16)
    h2t_ref[...] = h2f.T.astype(BF16)
    qp = jnp.dot(h2, wpq_ref[...], preferred_element_type=F32).astype(BF16)
    for c in range(PEER_QCOLS // PEER_HALF):
        cs = slice(c * PEER_HALF, (c + 1) * PEER_HALF)
        head, half = divmod(c, 2)
        out_ref = s2t_ref if half else s1t_ref
        scores = lax.dot_general(sk_ref[c], qp[:, cs], (((1,), (1,)), ((), ())), preferred_element_type=F32)
        for lt in range(scores.shape[1] // LANES):
            out_ref[lt, head * N_KEYS:(head + 1) * N_KEYS, :] = scores[:, lt * LANES:(lt + 1) * LANES]


def _finish(x2d, heads, conv_out, gates, wap_bf, wout_bf, n2w, wpq_bf, subk_bf, *, tm):
    n = x2d.shape[0]
    row = lambda i: (i, 0)
    return pl.pallas_call(
        _finish_kernel,
        grid=(n // tm,),
        in_specs=[
            pl.BlockSpec((tm, D_MODEL), row),
            pl.BlockSpec((tm, GROUP_WIDTH), row),
            pl.BlockSpec((tm, D_MODEL), row),
            pl.BlockSpec((tm, 2 * D_MODEL), row),
            _full((GROUP_WIDTH, D_MODEL)),
            _full((D_MODEL, D_MODEL)),
            _full((1, D_MODEL)),
            _full((D_MODEL, PEER_QCOLS)),
            _full((2 * PEER_HEADS, N_KEYS, PEER_HALF)),
        ],
        out_specs=[
            pl.BlockSpec((tm, D_MODEL), row),
            pl.BlockSpec((D_MODEL, tm), lambda i: (0, i)),
            pl.BlockSpec((tm // LANES, PEER_ROWS, LANES), lambda i: (i, 0, 0)),
            pl.BlockSpec((tm // LANES, PEER_ROWS, LANES), lambda i: (i, 0, 0)),
        ],
        out_shape=[
            jax.ShapeDtypeStruct((n, D_MODEL), F32),
            jax.ShapeDtypeStruct((D_MODEL, n), BF16),
            jax.ShapeDtypeStruct((n // LANES, PEER_ROWS, LANES), F32),
            jax.ShapeDtypeStruct((n // LANES, PEER_ROWS, LANES), F32),
        ],
        compiler_params=_params("parallel"),
        name="finish",
    )(x2d, heads, conv_out, gates, wap_bf, wout_bf, n2w, wpq_bf, subk_bf)


def _sorting_network(n):
    def merge(lo, hi, r):
        step = r * 2
        if step < hi - lo:
            yield from merge(lo, hi, step)
            yield from merge(lo + r, hi, step)
            yield from ((i, i + r) for i in range(lo + r, hi - r, step))
        else:
            yield (lo, lo + r)

    def sort(lo, hi):
        if hi > lo:
            mid = lo + (hi - lo) // 2
            yield from sort(lo, mid)
            yield from sort(mid + 1, hi)
            yield from merge(lo, hi, 1)

    return tuple(sort(0, n - 1))


_SORT16 = _sorting_network(PEER_TOPK)
VREGS_PER_KEYSET = N_KEYS // SUBLANES


def _replicated_max(x):
    return jnp.broadcast_to(jnp.max(x, axis=0, keepdims=True), x.shape)


def _replicated_min(x):
    return jnp.broadcast_to(jnp.min(x, axis=0, keepdims=True), x.shape)


def _top16(tiles):
    v = list(tiles)
    for a, b in _SORT16:
        v[a], v[b] = jnp.maximum(v[a], v[b]), jnp.minimum(v[a], v[b])
    tops = []
    for r in range(PEER_TOPK):
        m = _replicated_max(v[0])
        tops.append(m)
        hit = v[0] == m
        for k in range(PEER_TOPK - 1 - r):
            v[k] = jnp.where(hit, v[k + 1], v[k])
    return tops


def _peer_prep_kernel(s1t_ref, s2t_ref, phi_ref, c1_ref, e2_ref):
    sub = lax.broadcasted_iota(jnp.int32, (SUBLANES, LANES), 0)
    neg = jnp.full((SUBLANES, LANES), NEG_INF, F32)
    pos = jnp.full((SUBLANES, LANES), -NEG_INF, F32)

    def stack8(vals):
        out = vals[0]
        for b in range(1, SUBLANES):
            out = jnp.where(sub == b, vals[b], out)
        return out

    def head(h, carry):
        base = pl.multiple_of(h * N_KEYS, N_KEYS)
        rows = [pl.ds(base + k * SUBLANES, SUBLANES) for k in range(VREGS_PER_KEYSET)]
        s1 = [s1t_ref[r, :] for r in rows]
        s2 = [s2t_ref[r, :] for r in rows]
        top1 = _top16(s1)
        top2 = _top16(s2)
        a_lo, a_hi = stack8(top1[:SUBLANES]), stack8(top1[SUBLANES:])
        b_lo, b_hi = stack8(top2[:SUBLANES]), stack8(top2[SUBLANES:])
        cands = [top1[0] + b_lo, top1[0] + b_hi, top1[1] + b_lo, top1[2] + b_lo, top1[3] + b_lo]
        seen = sub < 4
        cands += [jnp.where(seen, neg, a_lo + top2[0]), a_hi + top2[0]]
        cands += [jnp.where(seen, neg, a_lo + top2[b]) for b in (1, 2, 3)]
        tops = _top16(cands + [neg] * (PEER_TOPK - len(cands)))
        best, tau = tops[0], tops[-1]
        z = jnp.zeros((SUBLANES, LANES), F32)
        for c in cands:
            z = z + jnp.where(c >= tau, jnp.exp(c - best), 0.0)
        inv_z = 1.0 / jnp.broadcast_to(jnp.sum(z, axis=0, keepdims=True), z.shape)
        phis = []
        for a in range(PEER_TOPK):
            t = jnp.where(top1[a] + b_lo >= tau, b_lo, pos)
            if a == 0:
                t = jnp.minimum(t, jnp.where(top1[a] + b_hi >= tau, b_hi, pos))
            phis.append(_replicated_min(t))
        for k, r in enumerate(rows):
            phi = pos
            for a in range(PEER_TOPK):
                phi = jnp.where(s1[k] == top1[a], phis[a], phi)
            phi_ref[r, :] = phi
            c1_ref[r, :] = jnp.exp(s1[k] - top1[0]) * inv_z
            e2_ref[r, :] = jnp.exp(s2[k] - top2[0])
        return carry

    def head_pair(hp, carry):
        head(2 * hp, carry)
        head(2 * hp + 1, carry)
        return carry

    lax.fori_loop(0, PEER_HEADS // 2, head_pair, 0)


def _peer_prep(s1t, s2t):
    slabs = s1t.shape[0]
    spec = pl.BlockSpec((None, PEER_ROWS, LANES), lambda i: (i, 0, 0))
    return pl.pallas_call(
        _peer_prep_kernel,
        grid=(slabs,),
        in_specs=[spec, spec],
        out_specs=[spec] * 3,
        out_shape=[jax.ShapeDtypeStruct((slabs, PEER_ROWS, LANES), F32)] * 3,
        compiler_params=_params("parallel"),
        name="peer_prep",
    )(s1t, s2t)


def _gelu(a):
    return 0.5 * a * (1.0 + lax.erf(a * np.float32(1.0 / np.sqrt(2.0))))


def _peer_dense_kernel(h2_ref, u_ref, vt_ref, phi_ref, c1_ref, s2_ref, e2_ref, x1_ref,
                       y_ref, acc_ref, at_ref, wg_ref, *, keys_per_step):
    j = pl.program_id(1)
    tn = h2_ref.shape[1]

    @pl.when(j == 0)
    def _():
        acc_ref[...] = jnp.zeros_like(acc_ref)

    at_ref[...] = jnp.dot(u_ref[...], h2_ref[...], preferred_element_type=F32)

    def first_key(ii, carry):
        i = j * keys_per_step + ii
        out0 = pl.multiple_of(ii * N_KEYS, N_KEYS)
        for lt in range(tn // LANES):
            lanes = slice(lt * LANES, (lt + 1) * LANES)
            w = [None] * VREGS_PER_KEYSET
            for h in range(PEER_HEADS):
                row_i = pl.ds(h * N_KEYS + i, 1)
                phi = jnp.broadcast_to(phi_ref[lt, row_i, :], (SUBLANES, LANES))
                c1 = jnp.broadcast_to(c1_ref[lt, row_i, :], (SUBLANES, LANES))
                for kb in range(VREGS_PER_KEYSET):
                    rows = slice(h * N_KEYS + kb * SUBLANES, h * N_KEYS + (kb + 1) * SUBLANES)
                    t = jnp.where(s2_ref[lt, rows, :] >= phi, e2_ref[lt, rows, :], 0.0) * c1
                    w[kb] = t if h == 0 else w[kb] + t
            for kp in range(VREGS_PER_KEYSET // 2):
                rows = pl.ds(out0 + kp * 2 * SUBLANES, 2 * SUBLANES)
                pair = jnp.concatenate([w[2 * kp], w[2 * kp + 1]], axis=0)
                wg_ref[rows, lanes] = (pair * _gelu(at_ref[rows, lanes])).astype(BF16)
        return carry

    lax.fori_loop(0, keys_per_step, first_key, 0)
    acc_ref[...] += jnp.dot(vt_ref[...], wg_ref[...], preferred_element_type=F32)

    @pl.when(j == pl.num_programs(1) - 1)
    def _():
        y_ref[...] = x1_ref[...] + acc_ref[...].T


def _peer_dense(h2, eu_bf, evt_bf, phi, c1, s2t, e2, x1, *, tn, keys_per_step):
    n = h2.shape[1]
    ec = keys_per_step * N_KEYS
    tok = lambda i, j: (i, 0)
    slab = pl.BlockSpec((tn // LANES, PEER_ROWS, LANES), lambda i, j: (i, 0, 0))
    return pl.pallas_call(
        functools.partial(_peer_dense_kernel, keys_per_step=keys_per_step),
        grid=(n // tn, N_EXPERTS // ec),
        in_specs=[
            pl.BlockSpec((D_MODEL, tn), lambda i, j: (0, i)),
            pl.BlockSpec((ec, D_MODEL), lambda i, j: (j, 0)),
            pl.BlockSpec((D_MODEL, ec), lambda i, j: (0, j)),
            slab, slab, slab, slab,
            pl.BlockSpec((tn, D_MODEL), tok),
        ],
        out_specs=pl.BlockSpec((tn, D_MODEL), tok),
        out_shape=jax.ShapeDtypeStruct((n, D_MODEL), F32),
        scratch_shapes=[pltpu.VMEM((D_MODEL, tn), F32),
                        pltpu.VMEM((ec, tn), F32),
                        pltpu.VMEM((ec, tn), BF16)],
        compiler_params=_params("parallel", "arbitrary"),
        name="peer_dense",
    )(h2, eu_bf, evt_bf, phi, c1, s2t, e2, x1)


def _token_tail(x2d, heads, conv_out, gates, lw, *, tm, tn, keys_per_step):
    x1, h2, s1t, s2t = _finish(x2d, heads, conv_out, gates, lw["wap"], lw["wout"], lw["n2"], lw["wpq"],
                               lw["subk"], tm=tm)
    phi, c1, e2 = _peer_prep(s1t, s2t)
    return _peer_dense(h2, lw["eu"], lw["evt"], phi, c1, s2t, e2, x1, tn=tn, keys_per_step=keys_per_step)


def kernel(x_prompt, x_sample, cache_kv_g1, cache_kv_g2, cache_kv_g3, state_conv,
           norm1_w, w_in, q_norm_w, k_norm_w, conv_w, conv_b, conv_ln_g, conv_ln_b,
           w_conv_proj, w_attn_proj, w_out, norm2_w, w_peer_q, peer_sub_keys,
           expert_u, expert_v):
    depth = w_in.shape[0]
    bsz, seq, _ = x_prompt.shape
    bd, n_new, _ = x_sample.shape
    past_len = cache_kv_g3.shape[2]
    caches = (cache_kv_g1, cache_kv_g2, cache_kv_g3)

    seg = _head_mean_matrix()
    cos_p, sin_p = _rope_tables(np.arange(seq))
    cos_s, sin_s = _rope_tables(past_len + np.arange(bd * n_new) % n_new)
    q_scale = HEAD_DIM ** -0.5

    hp = x_prompt.reshape(bsz * seq, D_MODEL)
    hs = x_sample.reshape(bd * n_new, D_MODEL)
    kv_p = [[], [], []]
    kv_s = [[], [], []]
    conv_p, conv_s = [], []
    for layer in range(depth):
        tile = lambda w: jnp.tile(w[layer], GROUP_HEADS)[None, :]
        lw = dict(
            wap=w_attn_proj[layer].astype(BF16), wout=w_out[layer].astype(BF16),
            n2=norm2_w[layer][None, :], wpq=w_peer_q[layer].astype(BF16),
            subk=peer_sub_keys[layer].reshape(2 * PEER_HEADS, N_KEYS, PEER_HALF).astype(BF16),
            eu=expert_u[layer].astype(BF16), evt=expert_v[layer].T.astype(BF16))
        w_in_bf = w_in[layer].astype(BF16)
        n1 = norm1_w[layer][None, :]
        qnw, knw = tile(q_norm_w), tile(k_norm_w)
        conv_args = (conv_w[layer], conv_b[layer][None, :], conv_ln_g[layer][None, :],
                     conv_ln_b[layer][None, :], w_conv_proj[layer].astype(BF16))

        q, k, v, u, gates, *by_res = _mixer_in(hp, seq, cos_p, sin_p, n1, w_in_bf, qnw, knw, seg,
                                               tm=256, q_dtype=BF16, q_scale=q_scale, by_residue=True)
        u3 = u.reshape(bsz, seq, CONV_DIM)
        c_out = _conv_prompt(u3, *conv_args, tm=512).reshape(bsz * seq, D_MODEL)
        q3, k3, v3 = (a.reshape(bsz, seq, ATTN_WIDTH) for a in (q, k, v))
        o1, l1 = _attn_prompt_group(q3, k3, v3, 0)
        o2, l2 = _attn_prompt_residue(*by_res[0:3], 1)
        o3, l3 = _attn_prompt_residue(*by_res[3:6], 2)
        heads = _combine_groups(o1, l1, o2, l2, o3, l3, tm=512)
        hp = _token_tail(hp, heads, c_out, gates, lw, tm=512, tn=512, keys_per_step=16)
        conv_p.append(u3[:, seq - (CONV_WIDTH - 1):])
        for g in range(N_GROUPS):
            keep = min(GROUP_WINDOWS[g], seq)
            gs = slice(g * GROUP_WIDTH, (g + 1) * GROUP_WIDTH)
            kv = jnp.stack([k3[:, seq - keep:, gs], v3[:, seq - keep:, gs]], axis=2)
            kv_p[g].append(kv.reshape(bsz, keep, 2, GROUP_HEADS, HEAD_DIM))

        q, k, v, u, gates = _mixer_in(hs, bd * n_new, cos_s, sin_s, n1, w_in_bf, qnw, knw, seg,
                                      tm=bd * n_new, q_dtype=F32, q_scale=q_scale)
        u3 = u.reshape(bd, n_new, CONV_DIM)
        xp = jnp.concatenate([state_conv[layer], u3], axis=1)
        c_out = _conv_sample(xp.transpose(1, 0, 2), *conv_args)
        c_out = c_out.transpose(1, 0, 2).reshape(bd * n_new, D_MODEL)
        heads = _attn_sample(q, k, v, *(c[layer] for c in caches), n_new=n_new)
        hs = _token_tail(hs, heads, c_out, gates, lw, tm=bd * n_new, tn=bd * n_new, keys_per_step=16)
        conv_s.append(xp[:, n_new:])
        k3, v3 = (a.reshape(bd, n_new, ATTN_WIDTH) for a in (k, v))
        for g in range(N_GROUPS):
            gs = slice(g * GROUP_WIDTH, (g + 1) * GROUP_WIDTH)
            kv = jnp.stack([k3[:, :, gs], v3[:, :, gs]], axis=2)
            kv_s[g].append(kv.reshape(bd, n_new, 2, GROUP_HEADS, HEAD_DIM))

    return (hp.reshape(bsz, seq, D_MODEL), hs.reshape(bd, n_new, D_MODEL),
            jnp.stack(kv_p[0]), jnp.stack(kv_p[1]), jnp.stack(kv_p[2]), jnp.stack(conv_p),
            jnp.stack(kv_s[0]), jnp.stack(kv_s[1]), jnp.stack(kv_s[2]), jnp.stack(conv_s))
```
